```python
import numpy as np
import jax
import jax.numpy as jnp
from jax import lax

D_MODEL = 4096
BATCH = 2
SEQ = 8192
DEPTH = 2

GRID_W = 64
CTX_LEN = 256
N_MIXERS = 2
N_A_LAYERS = (DEPTH + 1) // 2
N_B_LAYERS = DEPTH // 2
BLOCK = 128

A_HEAD_DIM = 64
A_HEADS = D_MODEL // A_HEAD_DIM
A_KV_HEADS = 8
A_GROUP = A_HEADS // A_KV_HEADS
WINDOW = 128

B_HEADS = 32
B_Q_RANK = 1024
B_KV_RANK = 512
B_NOPE = 128
B_ROPE = 64
B_V = 128

ROPE_BASE = 10000.0

N_GROUPS = 8
EXPERTS_PER_GROUP = 4
N_EXPERTS = N_GROUPS * EXPERTS_PER_GROUP
TOP_K = 2
D_EXPERT = 512
MOE_BLOCK = 128

EPS = 1e-6
NEG_INF = -1e30

kernel_name = "hybrid_dit_swa_sink_mla_hmoe"


def rmsnorm(x, g):
    xf = x.astype(jnp.float32)
    y = xf * lax.rsqrt(jnp.mean(xf * xf, axis=-1, keepdims=True) + EPS)
    return (y * g.astype(jnp.float32)).astype(x.dtype)


def modulate(x, g, shift, scale):
    xf = x.astype(jnp.float32)
    y = xf * lax.rsqrt(jnp.mean(xf * xf, axis=-1, keepdims=True) + EPS) * g.astype(jnp.float32)
    return (y * (1.0 + scale.astype(jnp.float32)) + shift.astype(jnp.float32)).astype(x.dtype)


def axial_rope(n_tok, d_rot):
    rows = n_tok // GRID_W
    row = jnp.broadcast_to(jnp.arange(rows, dtype=jnp.float32)[:, None], (rows, GRID_W)).reshape(-1)
    col = jnp.broadcast_to(jnp.arange(GRID_W, dtype=jnp.float32)[None, :], (rows, GRID_W)).reshape(-1)
    n_freq = d_rot // 4
    inv = ROPE_BASE ** (-jnp.arange(n_freq, dtype=jnp.float32) / n_freq)
    ang = jnp.concatenate([row[:, None] * inv, col[:, None] * inv], axis=-1)
    return jnp.cos(ang), jnp.sin(ang)


def apply_rope(x, cos, sin):
    half = x.shape[-1] // 2
    xf = x.astype(jnp.float32)
    x1, x2 = xf[..., :half], xf[..., half:]
    return jnp.concatenate([x1 * cos - x2 * sin, x2 * cos + x1 * sin], axis=-1).astype(x.dtype)


def sink_softmax(parts, sink):
    lead = parts[0].shape[:-1]
    sink_col = jnp.broadcast_to(sink.astype(jnp.float32)[None, :, :, None, None], lead + (1,))
    logits = jnp.concatenate([p.astype(jnp.float32) for p in parts] + [sink_col], axis=-1)
    probs = jax.nn.softmax(logits, axis=-1)
    cuts = [int(v) for v in np.cumsum([p.shape[-1] for p in parts])]
    return jnp.split(probs, cuts, axis=-1)[:-1]


def windowed_gqa(h_lat, h_ctx, w_qkv, w_o, sink, with_ctx_out):
    B, S, _ = h_lat.shape
    C = h_ctx.shape[1]
    qd, kvd = A_HEADS * A_HEAD_DIM, A_KV_HEADS * A_HEAD_DIM
    scale = A_HEAD_DIM ** -0.5
    sink_g = sink.reshape(A_KV_HEADS, A_GROUP)
    cos, sin = axial_rope(S, A_HEAD_DIM)
    z = h_lat @ w_qkv
    q = apply_rope(z[..., :qd].reshape(B, S, A_KV_HEADS, A_GROUP, A_HEAD_DIM), cos[:, None, None, :], sin[:, None, None, :])
    k = apply_rope(z[..., qd:qd + kvd].reshape(B, S, A_KV_HEADS, A_HEAD_DIM), cos[:, None, :], sin[:, None, :])
    v = z[..., qd + kvd:].reshape(B, S, A_KV_HEADS, A_HEAD_DIM)
    zc = h_ctx @ (w_qkv if with_ctx_out else w_qkv[:, qd:])
    kc = zc[..., -2 * kvd:-kvd].reshape(B, C, A_KV_HEADS, A_HEAD_DIM)
    vc = zc[..., -kvd:].reshape(B, C, A_KV_HEADS, A_HEAD_DIM)
    nb = S // BLOCK
    pad = ((0, 0), (BLOCK, BLOCK), (0, 0), (0, 0))
    k_pad, v_pad = jnp.pad(k, pad), jnp.pad(v, pad)
    q_blocks = q.reshape(B, nb, BLOCK, A_KV_HEADS, A_GROUP, A_HEAD_DIM).swapaxes(0, 1)
    q_off = jnp.arange(BLOCK)
    k_off = jnp.arange(3 * BLOCK) - BLOCK

    def block(args):
        b, qb = args
        start = b * BLOCK
        kb = lax.dynamic_slice_in_dim(k_pad, start, 3 * BLOCK, axis=1)
        vb = lax.dynamic_slice_in_dim(v_pad, start, 3 * BLOCK, axis=1)
        qpos = start + q_off
        kpos = start + k_off
        valid = (jnp.abs(qpos[:, None] - kpos[None, :]) <= WINDOW) & (kpos >= 0)[None, :] & (kpos < S)[None, :]
        s_loc = jnp.where(valid, jnp.einsum("bqkgd,bjkd->bkgqj", qb, kb).astype(jnp.float32) * scale, NEG_INF)
        s_ctx = jnp.einsum("bqkgd,bckd->bkgqc", qb, kc).astype(jnp.float32) * scale
        p_loc, p_ctx = sink_softmax([s_loc, s_ctx], sink_g)
        return (jnp.einsum("bkgqj,bjkd->bqkgd", p_loc.astype(vb.dtype), vb)
                + jnp.einsum("bkgqc,bckd->bqkgd", p_ctx.astype(vc.dtype), vc))

    o = lax.map(block, (jnp.arange(nb), q_blocks))
    o_lat = o.swapaxes(0, 1).reshape(B, S, qd) @ w_o
    if not with_ctx_out:
        return o_lat, None
    qc = zc[..., :qd].reshape(B, C, A_KV_HEADS, A_GROUP, A_HEAD_DIM)
    (p,) = sink_softmax([jnp.einsum("bqkgd,bckd->bkgqc", qc, kc).astype(jnp.float32) * scale], sink_g)
    o_ctx = jnp.einsum("bkgqc,bckd->bqkgd", p.astype(vc.dtype), vc).reshape(B, C, qd) @ w_o
    return o_lat, o_ctx


def mla(h_lat, h_ctx, w_dkv, g_q, g_kv, w_uq, w_ukv, w_o, with_ctx_out):
    B, S, _ = h_lat.shape
    scale = (B_NOPE + B_ROPE) ** -0.5
    w_ukv3 = w_ukv.reshape(B_KV_RANK, B_HEADS, B_NOPE + B_V)
    w_uk, w_uv = w_ukv3[..., :B_NOPE], w_ukv3[..., B_NOPE:]

    def q_part(zq):
        q = rmsnorm(zq, g_q) @ w_uq
        q = q.reshape(*zq.shape[:-1], B_HEADS, B_NOPE + B_ROPE)
        return q[..., :B_NOPE], q[..., B_NOPE:]

    def kv_part(zkv):
        return rmsnorm(zkv[..., :B_KV_RANK], g_kv), zkv[..., B_KV_RANK:]

    def attend(qn, qr, ckv, kr):
        q_lat = jnp.einsum("bqhn,lhn->bqhl", qn, w_uk)
        s = jnp.einsum("bqhl,bkl->bhqk", q_lat, ckv) + jnp.einsum("bqhr,bkr->bhqk", qr, kr)
        p = jax.nn.softmax(s.astype(jnp.float32) * scale, axis=-1).astype(ckv.dtype)
        o_lat = jnp.einsum("bhqk,bkl->bqhl", p, ckv)
        o = jnp.einsum("bqhl,lhv->bqhv", o_lat, w_uv)
        return o.reshape(*o.shape[:2], B_HEADS * B_V)

    cos, sin = axial_rope(S, B_ROPE)
    z = h_lat @ w_dkv
    qn, qr = q_part(z[..., :B_Q_RANK])
    qr = apply_rope(qr, cos[:, None, :], sin[:, None, :])
    ckv, kr = kv_part(z[..., B_Q_RANK:])
    kr = apply_rope(kr, cos, sin)
    zc = h_ctx @ (w_dkv if with_ctx_out else w_dkv[:, B_Q_RANK:])
    ckv_c, kr_c = kv_part(zc[..., -(B_KV_RANK + B_ROPE):])
    ckv_all = jnp.concatenate([ckv, ckv_c], axis=1)
    kr_all = jnp.concatenate([kr, kr_c], axis=1)
    nb = S // BLOCK
    qn_b = qn.reshape(B, nb, BLOCK, B_HEADS, B_NOPE).swapaxes(0, 1)
    qr_b = qr.reshape(B, nb, BLOCK, B_HEADS, B_ROPE).swapaxes(0, 1)
    o = lax.map(lambda a: attend(a[0], a[1], ckv_all, kr_all), (qn_b, qr_b))
    o_lat = o.swapaxes(0, 1).reshape(B, S, B_HEADS * B_V) @ w_o
    if not with_ctx_out:
        return o_lat, None
    qn_c, qr_c = q_part(zc[..., :B_Q_RANK])
    o_ctx = attend(qn_c, qr_c, ckv_c, kr_c) @ w_o
    return o_lat, o_ctx


def hier_moe(h, w_rg, b_rg, w_re, b_re, w_gu, w_dn):
    T, D = h.shape
    hf = h.astype(jnp.float32)
    lg = hf @ w_rg.astype(jnp.float32) + b_rg.astype(jnp.float32)
    grp = jnp.argmax(lg, axis=-1)
    p_grp = jnp.take_along_axis(jax.nn.softmax(lg, axis=-1), grp[:, None], axis=-1)
    le = (hf @ w_re.astype(jnp.float32) + b_re.astype(jnp.float32)).reshape(T, N_GROUPS, EXPERTS_PER_GROUP)
    le = jnp.take_along_axis(le, grp[:, None, None], axis=1)[:, 0]
    p_top, e_top = lax.top_k(jax.nn.softmax(le, axis=-1), TOP_K)
    gate = p_grp * p_top / jnp.sum(p_top, axis=-1, keepdims=True)
    eid = (grp[:, None] * EXPERTS_PER_GROUP + e_top).reshape(-1)
    n_assign = T * TOP_K
    order = jnp.argsort(eid)
    e_sorted = eid[order]
    tok_sorted = order // TOP_K
    w_sorted = gate.reshape(-1)[order]
    counts = jnp.zeros((N_EXPERTS,), jnp.int32).at[eid].add(1)
    padded = (counts + MOE_BLOCK - 1) // MOE_BLOCK * MOE_BLOCK
    p_end = jnp.cumsum(padded)
    dest = (p_end - padded)[e_sorted] + jnp.arange(n_assign) - (jnp.cumsum(counts) - counts)[e_sorted]
    n_rows = (n_assign + N_EXPERTS * (MOE_BLOCK - 1) + MOE_BLOCK - 1) // MOE_BLOCK * MOE_BLOCK
    n_blocks = n_rows // MOE_BLOCK
    buf = jnp.zeros((n_rows, D), h.dtype).at[dest].set(h[tok_sorted])
    blk_e = jnp.minimum(jnp.searchsorted(p_end, jnp.arange(n_blocks) * MOE_BLOCK, side="right"), N_EXPERTS - 1)

    def expert_block(args):
        xb, e = args
        gu = xb @ w_gu[e]
        return (jax.nn.silu(gu[:, :D_EXPERT]) * gu[:, D_EXPERT:]) @ w_dn[e]

    y_buf = lax.map(expert_block, (buf.reshape(n_blocks, MOE_BLOCK, D), blk_e)).reshape(n_rows, D)
    y = jnp.zeros((T, D), jnp.float32).at[tok_sorted].add(y_buf[dest].astype(jnp.float32) * w_sorted[:, None])
    return y.astype(h.dtype)


def setup_inputs(seed: int = 0) -> dict:
    key = jax.random.key(seed)
    ks = jax.random.split(key, 24)
    D = D_MODEL

    def nrm(k, shape, s=1.0):
        return jax.random.normal(k, shape, jnp.float32) * s

    qkv_cols = (A_HEADS + 2 * A_KV_HEADS) * A_HEAD_DIM
    return {
        "x": nrm(ks[0], (BATCH, SEQ, D)),
        "c": nrm(ks[1], (BATCH, D)),
        "ctx": nrm(ks[2], (BATCH, CTX_LEN, D)),
        "c_ctx": nrm(ks[3], (D,)),
        "ada_w": nrm(ks[4], (DEPTH, D, 6 * D), 0.5 * D ** -0.5),
        "ada_b": nrm(ks[5], (DEPTH, 6 * D), 0.02),
        "norm_g": 1.0 + nrm(ks[6], (DEPTH, 2, D), 0.02),
        "final_g": 1.0 + nrm(ks[7], (D,), 0.02),
        "a_wqkv": nrm(ks[8], (N_A_LAYERS, D, qkv_cols), D ** -0.5),
        "a_wo": nrm(ks[9], (N_A_LAYERS, A_HEADS * A_HEAD_DIM, D), (A_HEADS * A_HEAD_DIM) ** -0.5),
        "a_sink": nrm(ks[10], (N_A_LAYERS, A_HEADS), 1.0),
        "b_wdkv": nrm(ks[11], (N_B_LAYERS, D, B_Q_RANK + B_KV_RANK + B_ROPE), D ** -0.5),
        "b_gq": 1.0 + nrm(ks[12], (N_B_LAYERS, B_Q_RANK), 0.02),
        "b_gkv": 1.0 + nrm(ks[13], (N_B_LAYERS, B_KV_RANK), 0.02),
        "b_wuq": nrm(ks[14], (N_B_LAYERS, B_Q_RANK, B_HEADS * (B_NOPE + B_ROPE)), B_Q_RANK ** -0.5),
        "b_wukv": nrm(ks[15], (N_B_LAYERS, B_KV_RANK, B_HEADS * (B_NOPE + B_V)), B_KV_RANK ** -0.5),
        "b_wo": nrm(ks[16], (N_B_LAYERS, B_HEADS * B_V, D), (B_HEADS * B_V) ** -0.5),
        "r_wg": nrm(ks[17], (DEPTH, D, N_GROUPS), D ** -0.5),
        "r_bg": nrm(ks[18], (DEPTH, N_GROUPS), 0.01),
        "r_we": nrm(ks[19], (DEPTH, D, N_EXPERTS), D ** -0.5),
        "r_be": nrm(ks[20], (DEPTH, N_EXPERTS), 0.01),
        "e_wgu": nrm(ks[21], (DEPTH, N_EXPERTS, D, 2 * D_EXPERT), D ** -0.5),
        "e_wdn": nrm(ks[22], (DEPTH, N_EXPERTS, D_EXPERT, D), D_EXPERT ** -0.5),
    }


def reference(x, c, ctx, c_ctx, ada_w, ada_b, norm_g, final_g, a_wqkv, a_wo, a_sink,
              b_wdkv, b_gq, b_gkv, b_wuq, b_wukv, b_wo, r_wg, r_bg, r_we, r_be, e_wgu, e_wdn):
    B, S, D = x.shape
    C = ctx.shape[1]
    x_lat, x_ctx = x, ctx
    silu_c = jax.nn.silu(c)
    silu_cc = jax.nn.silu(c_ctx)
    for i in range(DEPTH):
        last = i == DEPTH - 1
        j = i // N_MIXERS
        mod_lat = (silu_c @ ada_w[i] + ada_b[i])[:, None, :]
        sh1, sc1, g1, sh2, sc2, g2 = jnp.split(mod_lat, 6, axis=-1)
        n_ctx_mod = 3 if last else 6
        mod_ctx = silu_cc @ ada_w[i][:, :n_ctx_mod * D] + ada_b[i][:n_ctx_mod * D]
        cm = jnp.split(mod_ctx, n_ctx_mod)
        h_lat = modulate(x_lat, norm_g[i, 0], sh1, sc1)
        h_ctx = modulate(x_ctx, norm_g[i, 0], cm[0], cm[1])
        if i % N_MIXERS == 0:
            o_lat, o_ctx = windowed_gqa(h_lat, h_ctx, a_wqkv[j], a_wo[j], a_sink[j], not last)
        else:
            o_lat, o_ctx = mla(h_lat, h_ctx, b_wdkv[j], b_gq[j], b_gkv[j], b_wuq[j], b_wukv[j], b_wo[j], not last)
        x_lat = x_lat + g1 * o_lat
        h_lat2 = modulate(x_lat, norm_g[i, 1], sh2, sc2)
        if last:
            y = hier_moe(h_lat2.reshape(B * S, D), r_wg[i], r_bg[i], r_we[i], r_be[i], e_wgu[i], e_wdn[i])
            x_lat = x_lat + g2 * y.reshape(B, S, D)
        else:
            x_ctx = x_ctx + cm[2] * o_ctx
            h_ctx2 = modulate(x_ctx, norm_g[i, 1], cm[3], cm[4])
            tokens = jnp.concatenate([h_lat2.reshape(B * S, D), h_ctx2.reshape(B * C, D)], axis=0)
            y = hier_moe(tokens, r_wg[i], r_bg[i], r_we[i], r_be[i], e_wgu[i], e_wdn[i])
            x_lat = x_lat + g2 * y[:B * S].reshape(B, S, D)
            x_ctx = x_ctx + cm[5] * y[B * S:].reshape(B, C, D)
    return rmsnorm(x_lat, final_g)
```

```python
import functools
import math

import jax
import jax.numpy as jnp
from jax import lax
from jax.experimental import pallas as pl
from jax.experimental.pallas import tpu as pltpu

F32 = jnp.float32
BF16 = jnp.bfloat16

EPS = 1e-6
NEG_INF = -1e30
ROPE_BASE = 10000.0
GRID_W = 64
A_HEAD_DIM = 64
WINDOW = 128
MOE_BLOCK = 128
LANES = 128
MOD_ROWS = 8
VMEM_LIMIT_BYTES = 56 * 1024 * 1024


def _cparams(*sem):
    return pltpu.CompilerParams(dimension_semantics=sem, vmem_limit_bytes=VMEM_LIMIT_BYTES)


def _dot(a, b):
    return jnp.dot(a, b, preferred_element_type=F32)


def _dot_nt(a, b):
    return lax.dot_general(a, b, (((1,), (1,)), ((), ())), preferred_element_type=F32)


def _split_bf16(v):
    hi = v.astype(BF16)
    lo = (v - hi.astype(F32)).astype(BF16)
    return hi, lo


def _rms(x):
    return x * lax.rsqrt(jnp.mean(x * x, axis=-1, keepdims=True) + EPS)


def _modulate_val(x, g, shift, scale):
    return _rms(x) * g * (1.0 + scale) + shift


def _rope128(x, cos, sin_signed):
    lane = lax.broadcasted_iota(jnp.int32, x.shape, 1)
    first_half = (lane & 32) == 0
    partner = jnp.where(first_half, pltpu.roll(x, 96, 1), pltpu.roll(x, 32, 1))
    return x * cos + partner * sin_signed


def _ada_kernel(c_ref, w_ref, b_ref, o_ref):
    c = c_ref[...]
    s = c * jax.nn.sigmoid(c)
    s_hi, s_lo = _split_bf16(s)
    w_hi, w_lo = _split_bf16(w_ref[...])
    r = _dot(jnp.concatenate([s_hi, s_lo], axis=0), w_hi)
    o_ref[...] = r[:MOD_ROWS] + r[MOD_ROWS:] + _dot(s_hi, w_lo) + b_ref[...]


def _ada(cc, ada_w, ada_b):
    depth, d, n = ada_w.shape
    tn = 512
    return pl.pallas_call(
        _ada_kernel,
        grid=(depth, n // tn),
        in_specs=[
            pl.BlockSpec((MOD_ROWS, d), lambda l, j: (0, 0)),
            pl.BlockSpec((None, d, tn), lambda l, j: (l, 0, j)),
            pl.BlockSpec((None, 1, tn), lambda l, j: (l, 0, j)),
        ],
        out_specs=pl.BlockSpec((None, MOD_ROWS, tn), lambda l, j: (l, 0, j)),
        out_shape=jax.ShapeDtypeStruct((depth, MOD_ROWS, n), F32),
        compiler_params=_cparams("arbitrary", "arbitrary"),
        name="ada",
    )(cc, ada_w, ada_b.reshape(depth, 1, n))


class _Dims:
    def __init__(self, b, s, c, d):
        self.b, self.s, self.c, self.d = b, s, c, d
        self.t_lat, self.t_ctx = b * s, b * c
        self.t_all = self.t_lat + self.t_ctx
        self.tm = next(t for t in (512, 256, 128) if s % t == 0 and self.t_ctx % t == 0)

    def modrow(self, i, tm):
        return jnp.minimum(i // (self.s // tm), self.b)

    def ropeblk(self, i, tm):
        return jnp.where(i < self.t_lat // tm, i % (self.s // tm), self.s // tm)

    def mod_spec(self, col, tm, width=None):
        width = self.d if width is None else width
        per = self.d // width
        if width == self.d:
            return pl.BlockSpec((1, 1, width), lambda i, *_: (self.modrow(i, tm), 0, col))
        return pl.BlockSpec((1, 1, width), lambda i, j, *_: (self.modrow(i, tm), 0, col * per + j))


def _rope_tables(s, d_rot, pad_rows):
    rows = s // GRID_W
    row = jnp.repeat(jnp.arange(rows, dtype=F32), GRID_W)
    col = jnp.tile(jnp.arange(GRID_W, dtype=F32), rows)
    n_freq = d_rot // 4
    inv = ROPE_BASE ** (-jnp.arange(n_freq, dtype=F32) / n_freq)
    ang = jnp.concatenate([row[:, None] * inv, col[:, None] * inv], axis=-1)
    cos, sin = jnp.cos(ang), jnp.sin(ang)
    reps = LANES // d_rot
    cos_t = jnp.tile(cos, (1, 2 * reps))
    sin_t = jnp.tile(jnp.concatenate([-sin, sin], axis=-1), (1, reps))
    cos_t = jnp.concatenate([cos_t, jnp.ones((pad_rows, LANES), F32)], axis=0)
    sin_t = jnp.concatenate([sin_t, jnp.zeros((pad_rows, LANES), F32)], axis=0)
    return cos_t, sin_t


def _modulate_kernel(x_ref, g_ref, sh_ref, sc_ref, o_ref):
    o_ref[...] = _modulate_val(x_ref[...], g_ref[...], sh_ref[0], sc_ref[0]).astype(o_ref.dtype)


def _modulate(dm, x, g, mod3, shift_col, scale_col):
    tm = min(dm.tm, 256)
    return pl.pallas_call(
        _modulate_kernel,
        grid=(dm.t_all // tm,),
        in_specs=[
            pl.BlockSpec((tm, dm.d), lambda i: (i, 0)),
            pl.BlockSpec((1, dm.d), lambda i: (0, 0)),
            dm.mod_spec(shift_col, tm),
            dm.mod_spec(scale_col, tm),
        ],
        out_specs=pl.BlockSpec((tm, dm.d), lambda i: (i, 0)),
        out_shape=jax.ShapeDtypeStruct((dm.t_all, dm.d), BF16),
        compiler_params=_cparams("arbitrary"),
        name="modulate",
    )(x, g.reshape(1, dm.d), mod3, mod3)


def _qkv_kernel(a_ref, w_ref, cos_ref, sin_ref, o_ref, *, n_q_blocks, n_rope_blocks, qscale):
    j = pl.program_id(1)
    acc = _dot(a_ref[...], w_ref[...])

    @pl.when(j >= n_rope_blocks)
    def _():
        o_ref[...] = acc.astype(o_ref.dtype)

    @pl.when(j < n_rope_blocks)
    def _():
        scale = jnp.where(j < n_q_blocks, qscale, 1.0).astype(F32)
        cos, sin = cos_ref[...], sin_ref[...]
        for g in range(acc.shape[1] // LANES):
            sl = slice(g * LANES, (g + 1) * LANES)
            o_ref[:, sl] = (_rope128(acc[:, sl], cos, sin) * scale).astype(o_ref.dtype)


def _qkv_proj(dm, h, w, cos_t, sin_t, qd, kvd):
    tm = dm.tm
    n = w.shape[1]
    tn = math.gcd(512, kvd)
    kern = functools.partial(_qkv_kernel, n_q_blocks=qd // tn, n_rope_blocks=(qd + kvd) // tn,
                             qscale=A_HEAD_DIM ** -0.5)
    return pl.pallas_call(
        kern,
        grid=(dm.t_all // tm, n // tn),
        in_specs=[
            pl.BlockSpec((tm, dm.d), lambda i, j: (i, 0)),
            pl.BlockSpec((dm.d, tn), lambda i, j: (0, j)),
            pl.BlockSpec((tm, LANES), lambda i, j: (dm.ropeblk(i, tm), 0)),
            pl.BlockSpec((tm, LANES), lambda i, j: (dm.ropeblk(i, tm), 0)),
        ],
        out_specs=pl.BlockSpec((tm, tn), lambda i, j: (i, j)),
        out_shape=jax.ShapeDtypeStruct((dm.t_all, n), BF16),
        compiler_params=_cparams("arbitrary", "arbitrary"),
        name="qkv_proj",
    )(h, w, cos_t, sin_t)


def _attn_a_kernel(sink_ref, q_ref, kp_ref, kc_ref, kn_ref, vp_ref, vc_ref, vn_ref, kx_ref, vx_ref, o_ref,
                   *, nb, group):
    i = pl.program_id(1)
    hp = pl.program_id(2)
    npair = group // 2
    w = WINDOW
    lane = lax.broadcasted_iota(jnp.int32, (1, LANES), 1)
    lo_mask = lane < A_HEAD_DIM

    k_loc = jnp.concatenate([kp_ref[...], kc_ref[...], kn_ref[...]], axis=0).astype(F32)
    v_loc = jnp.concatenate([vp_ref[...], vc_ref[...], vn_ref[...]], axis=0).astype(F32)
    k_ctx = kx_ref[...].astype(F32)
    v_ctx = vx_ref[...].astype(F32)

    r = lax.broadcasted_iota(jnp.int32, (w, 3 * w), 0)
    c = lax.broadcasted_iota(jnp.int32, (w, 3 * w), 1)
    rel = c - w - r
    valid = ((rel >= -w) & (rel <= w) & ((c >= w) | (i > 0)) & ((c < 2 * w) | (i < nb - 1)) & (i < nb))
    bias = jnp.where(valid, 0.0, NEG_INF).astype(F32)

    for par in range(2):
        def split(t):
            if par == 0:
                left = jnp.where(lo_mask, t, 0.0)
                right = pltpu.roll(left, A_HEAD_DIM, 1)
            else:
                right = jnp.where(lo_mask, 0.0, t)
                left = pltpu.roll(right, A_HEAD_DIM, 1)
            return left.astype(BF16), right.astype(BF16)

        k_loc_lr, k_ctx_lr = split(k_loc), split(k_ctx)
        v_loc_lr, v_ctx_lr = split(v_loc), split(v_ctx)
        base = par * group * A_HEAD_DIM
        qs = jnp.concatenate([q_ref[:, base + pp * LANES: base + (pp + 1) * LANES] for pp in range(npair)], axis=0)

        o_pair = None
        for side in range(2):
            s_loc = _dot_nt(qs, k_loc_lr[side])
            s_ctx = _dot_nt(qs, k_ctx_lr[side])
            e_loc, e_ctx, dens = [], [], []
            for pp in range(npair):
                sink = sink_ref[(2 * hp + par) * group + 2 * pp + side]
                sl = s_loc[pp * w:(pp + 1) * w] + bias
                sc = s_ctx[pp * w:(pp + 1) * w]
                m = jnp.maximum(jnp.maximum(jnp.max(sl, axis=1, keepdims=True),
                                            jnp.max(sc, axis=1, keepdims=True)), sink)
                el = jnp.exp(sl - m)
                ec = jnp.exp(sc - m)
                dens.append(jnp.sum(el, axis=1, keepdims=True) + jnp.sum(ec, axis=1, keepdims=True)
                            + jnp.exp(sink - m))
                e_loc.append(el.astype(BF16))
                e_ctx.append(ec.astype(BF16))
            o = _dot(jnp.concatenate(e_loc, axis=0), v_loc_lr[side]) + _dot(jnp.concatenate(e_ctx, axis=0),
                                                                              v_ctx_lr[side])
            o = o / jnp.concatenate(dens, axis=0)
            o_pair = o if o_pair is None else o_pair + o
        for pp in range(npair):
            o_ref[:, base + pp * LANES: base + (pp + 1) * LANES] = o_pair[pp * w:(pp + 1) * w].astype(o_ref.dtype)


def _attn_a(dm, z, sink, n_heads, n_kv):
    group = n_heads // n_kv
    qd, kvd = n_heads * A_HEAD_DIM, n_kv * A_HEAD_DIM
    w = WINDOW
    nb = dm.s // w
    ncb = dm.c // w
    lat_blocks = dm.t_lat // w
    qw = 2 * group * A_HEAD_DIM
    kcol, vcol = qd // LANES, (qd + kvd) // LANES

    def qrow(b, i):
        return jnp.where(i < nb, b * nb + i, lat_blocks + b * ncb + (i - nb))

    def krow(b, i, off):
        return b * nb + jnp.clip(jnp.minimum(i, nb - 1) + off, 0, nb - 1)

    def kv_spec(col, off):
        return pl.BlockSpec((w, LANES), lambda b, i, hp, s_: (krow(b, i, off), col + hp))

    def ctx_spec(col):
        return pl.BlockSpec((dm.c, LANES), lambda b, i, hp, s_: (dm.t_lat // dm.c + b, col + hp))

    grid_spec = pltpu.PrefetchScalarGridSpec(
        num_scalar_prefetch=1,
        grid=(dm.b, nb + ncb, n_kv // 2),
        in_specs=[
            pl.BlockSpec((w, qw), lambda b, i, hp, s_: (qrow(b, i), hp)),
            kv_spec(kcol, -1), kv_spec(kcol, 0), kv_spec(kcol, 1),
            kv_spec(vcol, -1), kv_spec(vcol, 0), kv_spec(vcol, 1),
            ctx_spec(kcol), ctx_spec(vcol),
        ],
        out_specs=pl.BlockSpec((w, qw), lambda b, i, hp, s_: (qrow(b, i), hp)),
    )
    return pl.pallas_call(
        functools.partial(_attn_a_kernel, nb=nb, group=group),
        grid_spec=grid_spec,
        out_shape=jax.ShapeDtypeStruct((dm.t_all, qd), BF16),
        compiler_params=_cparams("arbitrary", "arbitrary", "arbitrary"),
        name="attn_window",
    )(sink, z, z, z, z, z, z, z, z, z)


def _mm_res_kernel(a_ref, w_ref, res_ref, gate_ref, o_ref):
    o_ref[...] = res_ref[...] + gate_ref[0] * _dot(a_ref[...], w_ref[...])


def _mm_res(dm, a, w, res, mod3, gate_col, rows):
    tm = dm.tm
    k, n = w.shape
    tn = min(512, n)
    return pl.pallas_call(
        _mm_res_kernel,
        grid=(rows // tm, n // tn),
        in_specs=[
            pl.BlockSpec((tm, k), lambda i, j: (i, 0)),
            pl.BlockSpec((k, tn), lambda i, j: (0, j)),
            pl.BlockSpec((tm, tn), lambda i, j: (i, j)),
            dm.mod_spec(gate_col, tm, tn),
        ],
        out_specs=pl.BlockSpec((tm, tn), lambda i, j: (i, j)),
        out_shape=jax.ShapeDtypeStruct((rows, n), F32),
        compiler_params=_cparams("arbitrary", "arbitrary"),
        name="out_proj_residual",
    )(a, w, res, mod3)


def _mod_router_kernel(x_ref, g_ref, sh_ref, sc_ref, wh_ref, wl_ref, br_ref, h_ref, route_ref, *, n_groups, n_experts):
    h = _modulate_val(x_ref[...], g_ref[...], sh_ref[0], sc_ref[0])
    h_ref[...] = h
    h_hi, h_lo = _split_bf16(h)
    wh = wh_ref[...]
    lg = _dot(h_hi, wh) + _dot(h_lo, wh) + _dot(h_hi, wl_ref[...]) + br_ref[...]

    per = n_experts // n_groups
    lane = lax.broadcasted_iota(jnp.int32, lg.shape, 1)
    lane_f = lane.astype(F32)
    big = float(LANES)

    def first_argmax(v, vmax):
        return jnp.min(jnp.where(v == vmax, lane_f, big), axis=1, keepdims=True)

    is_g = lane < n_groups
    g_l = jnp.where(is_g, lg, NEG_INF)
    gmax = jnp.max(g_l, axis=1, keepdims=True)
    grp = first_argmax(g_l, gmax)
    p_grp = 1.0 / jnp.sum(jnp.where(is_g, jnp.exp(g_l - gmax), 0.0), axis=1, keepdims=True)

    lo = n_groups + grp * per
    e_l = jnp.where((lane_f >= lo) & (lane_f < lo + per), lg, NEG_INF)
    t1 = jnp.max(e_l, axis=1, keepdims=True)
    i1 = first_argmax(e_l, t1)
    e_l2 = jnp.where(lane_f == i1, NEG_INF, e_l)
    t2 = jnp.max(e_l2, axis=1, keepdims=True)
    i2 = first_argmax(e_l2, t2)
    ratio = jnp.exp(t2 - t1)
    gate1 = p_grp / (1.0 + ratio)
    gate2 = gate1 * ratio
    route = jnp.where(lane == 0, i1 - n_groups,
                      jnp.where(lane == 1, i2 - n_groups,
                                jnp.where(lane == 2, gate1, jnp.where(lane == 3, gate2, 0.0))))
    route_ref[...] = route


def _mod_router(dm, x, g, mod3, shift_col, scale_col, w_r, b_r, rows, n_groups, n_experts):
    tm = min(dm.tm, 256)
    w_hi, w_lo = _split_bf16(w_r)
    kern = functools.partial(_mod_router_kernel, n_groups=n_groups, n_experts=n_experts)
    return pl.pallas_call(
        kern,
        grid=(rows // tm,),
        in_specs=[
            pl.BlockSpec((tm, dm.d), lambda i: (i, 0)),
            pl.BlockSpec((1, dm.d), lambda i: (0, 0)),
            dm.mod_spec(shift_col, tm),
            dm.mod_spec(scale_col, tm),
            pl.BlockSpec((dm.d, LANES), lambda i: (0, 0)),
            pl.BlockSpec((dm.d, LANES), lambda i: (0, 0)),
            pl.BlockSpec((1, LANES), lambda i: (0, 0)),
        ],
        out_specs=[
            pl.BlockSpec((tm, dm.d), lambda i: (i, 0)),
            pl.BlockSpec((tm, LANES), lambda i: (i, 0)),
        ],
        out_shape=[
            jax.ShapeDtypeStruct((rows, dm.d), F32),
            jax.ShapeDtypeStruct((rows, LANES), F32),
        ],
        compiler_params=_cparams("arbitrary"),
        name="modulate_router",
    )(x, g.reshape(1, dm.d), mod3, mod3, w_hi, w_lo, b_r)


def _dispatch(route, n_experts):
    rows = route.shape[0]
    eid = route[:, :2].astype(jnp.int32).reshape(-1)
    n_assign = eid.shape[0]
    order = jnp.argsort(eid)
    e_sorted = eid[order]
    counts = jnp.zeros((n_experts,), jnp.int32).at[eid].add(1)
    padded = (counts + MOE_BLOCK - 1) // MOE_BLOCK * MOE_BLOCK
    p_end = jnp.cumsum(padded)
    c_start = jnp.cumsum(counts) - counts
    dest_sorted = (p_end - padded)[e_sorted] + jnp.arange(n_assign, dtype=jnp.int32) - c_start[e_sorted]
    n_rows = (n_assign + n_experts * (MOE_BLOCK - 1) + MOE_BLOCK - 1) // MOE_BLOCK * MOE_BLOCK
    n_blocks = n_rows // MOE_BLOCK
    src_tok = jnp.zeros((n_rows,), jnp.int32).at[dest_sorted].set((order // 2).astype(jnp.int32))
    dest = jnp.zeros((n_assign,), jnp.int32).at[order].set(dest_sorted.astype(jnp.int32))
    blk_e = jnp.minimum(jnp.searchsorted(p_end, jnp.arange(n_blocks, dtype=jnp.int32) * MOE_BLOCK, side="right"),
                        n_experts - 1).astype(jnp.int32)
    n_used = (p_end[-1:] // MOE_BLOCK).astype(jnp.int32)
    del rows
    return blk_e, src_tok, n_used, dest, n_rows


def _moe_kernel(blk_e_ref, src_ref, nused_ref, h_hbm, wgu_ref, wdn_ref, y_ref, xb, sem, *, d_exp):
    del blk_e_ref
    b = pl.program_id(0)
    n_used = nused_ref[0]
    slot = b % 2

    def row_copies(blk, slot_):
        return [pltpu.make_async_copy(h_hbm.at[pl.ds(src_ref[blk * MOE_BLOCK + r], 1), :],
                                      xb.at[slot_, pl.ds(r, 1), :], sem.at[slot_])
                for r in range(MOE_BLOCK)]

    @pl.when(b == 0)
    def _():
        for cp in row_copies(0, 0):
            cp.start()

    @pl.when(b >= n_used)
    def _():
        y_ref[...] = jnp.zeros(y_ref.shape, y_ref.dtype)

    @pl.when(b < n_used)
    def _():
        for cp in row_copies(b, slot):
            cp.wait()

        @pl.when(b + 1 < n_used)
        def _():
            for cp in row_copies(b + 1, 1 - slot):
                cp.start()

        xrow = xb[slot].astype(BF16)
        gu = _dot(xrow, wgu_ref[...])
        g, u = gu[:, :d_exp], gu[:, d_exp:]
        act = (g * jax.nn.sigmoid(g) * u).astype(BF16)
        y_ref[...] = _dot(act, wdn_ref[...])


def _moe_experts(h, blk_e, src_tok, n_used, w_gu, w_dn, n_rows):
    n_exp, d, f2 = w_gu.shape
    d_exp = f2 // 2
    n_blocks = n_rows // MOE_BLOCK
    grid_spec = pltpu.PrefetchScalarGridSpec(
        num_scalar_prefetch=3,
        grid=(n_blocks,),
        in_specs=[
            pl.BlockSpec(memory_space=pl.ANY),
            pl.BlockSpec((None, d, f2), lambda b, be, st, nu: (be[b], 0, 0)),
            pl.BlockSpec((None, d_exp, d), lambda b, be, st, nu: (be[b], 0, 0)),
        ],
        out_specs=pl.BlockSpec((MOE_BLOCK, d), lambda b, be, st, nu: (b, 0)),
        scratch_shapes=[
            pltpu.VMEM((2, MOE_BLOCK, d), F32),
            pltpu.SemaphoreType.DMA((2,)),
        ],
    )
    return pl.pallas_call(
        functools.partial(_moe_kernel, d_exp=d_exp),
        grid_spec=grid_spec,
        out_shape=jax.ShapeDtypeStruct((n_rows, d), F32),
        compiler_params=_cparams("arbitrary"),
        name="moe_experts",
    )(blk_e, src_tok, n_used, h, w_gu, w_dn)


def _combine_kernel(dest_ref, y_hbm, x_ref, route_ref, g2_ref, *refs, tc, n_blocks, final):
    if final:
        fg_ref, out_ref, yb, sem = refs
    else:
        ng_ref, sh_ref, sc_ref, xo_ref, h_ref, yb, sem = refs
    t = pl.program_id(0)
    slot = t % 2

    def row_copies(blk, slot_):
        return [pltpu.make_async_copy(y_hbm.at[pl.ds(dest_ref[(blk * tc + r) * 2 + k], 1), :],
                                      yb.at[slot_, k, pl.ds(r, 1), :], sem.at[slot_])
                for r in range(tc) for k in range(2)]

    @pl.when(t == 0)
    def _():
        for cp in row_copies(0, 0):
            cp.start()

    for cp in row_copies(t, slot):
        cp.wait()

    @pl.when(t + 1 < n_blocks)
    def _():
        for cp in row_copies(t + 1, 1 - slot):
            cp.start()

    route = route_ref[...]
    y = route[:, 2:3] * yb[slot, 0] + route[:, 3:4] * yb[slot, 1]
    xn = x_ref[...] + g2_ref[0] * y
    if final:
        out_ref[...] = _rms(xn) * fg_ref[...]
    else:
        xo_ref[...] = xn
        h_ref[...] = _modulate_val(xn, ng_ref[...], sh_ref[0], sc_ref[0]).astype(h_ref.dtype)


def _combine(dm, y_buf, dest, x, route, mod3, gate_col, rows, *, final_g=None, next_g=None, next_mod3=None):
    tc = MOE_BLOCK
    n_blocks = rows // tc
    final = final_g is not None
    row_spec = pl.BlockSpec((tc, dm.d), lambda t, d_: (t, 0))
    vec_spec = pl.BlockSpec((1, dm.d), lambda t, d_: (0, 0))
    in_specs = [
        pl.BlockSpec(memory_space=pl.ANY),
        row_spec,
        pl.BlockSpec((tc, LANES), lambda t, d_: (t, 0)),
        dm.mod_spec(gate_col, tc),
    ]
    if final:
        in_specs += [vec_spec]
        args = (final_g.reshape(1, dm.d),)
        out_specs = row_spec
        out_shape = jax.ShapeDtypeStruct((rows, dm.d), F32)
    else:
        in_specs += [vec_spec, dm.mod_spec(0, tc), dm.mod_spec(1, tc)]
        args = (next_g.reshape(1, dm.d), next_mod3, next_mod3)
        out_specs = [row_spec, row_spec]
        out_shape = [jax.ShapeDtypeStruct((rows, dm.d), F32), jax.ShapeDtypeStruct((rows, dm.d), BF16)]
    grid_spec = pltpu.PrefetchScalarGridSpec(
        num_scalar_prefetch=1,
        grid=(n_blocks,),
        in_specs=in_specs,
        out_specs=out_specs,
        scratch_shapes=[
            pltpu.VMEM((2, 2, tc, dm.d), F32),
            pltpu.SemaphoreType.DMA((2,)),
        ],
    )
    return pl.pallas_call(
        functools.partial(_combine_kernel, tc=tc, n_blocks=n_blocks, final=final),
        grid_spec=grid_spec,
        out_shape=out_shape,
        compiler_params=_cparams("arbitrary"),
        name="moe_combine",
    )(dest, y_buf, x, route, mod3, *args)


def _dkv_kernel(a_ref, wq_ref, wkv_ref, gq_ref, gkv_ref, cos_ref, sin_ref, zq_ref, ckv_ref, kr_ref, *, kv_rank):
    a = a_ref[...]
    zq_ref[...] = (_rms(_dot(a, wq_ref[...])) * gq_ref[...]).astype(zq_ref.dtype)
    zkv = _dot(a, wkv_ref[...])
    ckv_ref[...] = (_rms(zkv[:, :kv_rank]) * gkv_ref[...]).astype(ckv_ref.dtype)
    kr_ref[...] = _rope128(zkv[:, kv_rank:], cos_ref[...], sin_ref[...]).astype(kr_ref.dtype)


def _dkv_proj(dm, h, wq, wkv, gq, gkv, cos_t, sin_t):
    tm = dm.tm
    q_rank = wq.shape[1]
    kv_rank = wkv.shape[1] - LANES
    rope_spec = pl.BlockSpec((tm, LANES), lambda i: (dm.ropeblk(i, tm), 0))
    return pl.pallas_call(
        functools.partial(_dkv_kernel, kv_rank=kv_rank),
        grid=(dm.t_all // tm,),
        in_specs=[
            pl.BlockSpec((tm, dm.d), lambda i: (i, 0)),
            pl.BlockSpec((dm.d, q_rank), lambda i: (0, 0)),
            pl.BlockSpec((dm.d, kv_rank + LANES), lambda i: (0, 0)),
            pl.BlockSpec((1, q_rank), lambda i: (0, 0)),
            pl.BlockSpec((1, kv_rank), lambda i: (0, 0)),
            rope_spec, rope_spec,
        ],
        out_specs=[
            pl.BlockSpec((tm, q_rank), lambda i: (i, 0)),
            pl.BlockSpec((tm, kv_rank), lambda i: (i, 0)),
            pl.BlockSpec((tm, LANES), lambda i: (i, 0)),
        ],
        out_shape=[
            jax.ShapeDtypeStruct((dm.t_all, q_rank), BF16),
            jax.ShapeDtypeStruct((dm.t_all, kv_rank), BF16),
            jax.ShapeDtypeStruct((dm.t_all, LANES), BF16),
        ],
        compiler_params=_cparams("arbitrary"),
        name="mla_down_proj",
    )(h, wq, wkv, gq.reshape(1, -1), gkv.reshape(1, -1), cos_t, sin_t)


def _qup_kernel(a_ref, w_ref, cos_ref, sin_ref, o_ref, *, qscale):
    acc = _dot(a_ref[...], w_ref[...])
    cos, sin = cos_ref[...], sin_ref[...]
    for g in range(acc.shape[1] // LANES):
        sl = slice(g * LANES, (g + 1) * LANES)
        x = acc[:, sl]
        if g % 2 == 1:
            x = _rope128(x, cos, sin)
        o_ref[:, sl] = (x * qscale).astype(o_ref.dtype)


def _qup_proj(dm, zq, w, cos_t, sin_t, qscale):
    tm = dm.tm
    k, n = w.shape
    tn = min(1024, n)
    rope_spec = pl.BlockSpec((tm, LANES), lambda i, j: (dm.ropeblk(i, tm), 0))
    return pl.pallas_call(
        functools.partial(_qup_kernel, qscale=qscale),
        grid=(dm.t_lat // tm, n // tn),
        in_specs=[
            pl.BlockSpec((tm, k), lambda i, j: (i, 0)),
            pl.BlockSpec((k, tn), lambda i, j: (0, j)),
            rope_spec, rope_spec,
        ],
        out_specs=pl.BlockSpec((tm, tn), lambda i, j: (i, j)),
        out_shape=jax.ShapeDtypeStruct((dm.t_lat, n), BF16),
        compiler_params=_cparams("arbitrary", "arbitrary"),
        name="mla_q_up",
    )(zq, w, cos_t, sin_t)


def _kvup_kernel(a_ref, w_ref, kr_ref, k_ref, v_ref, *, heads):
    acc = _dot(a_ref[...], w_ref[...])
    kr = kr_ref[...]
    for h in range(heads):
        k_ref[:, h * 2 * LANES: h * 2 * LANES + LANES] = acc[:, h * 2 * LANES: h * 2 * LANES + LANES].astype(
            k_ref.dtype)
        k_ref[:, h * 2 * LANES + LANES: (h + 1) * 2 * LANES] = kr
        v_ref[:, h * LANES: (h + 1) * LANES] = acc[:, h * 2 * LANES + LANES: (h + 1) * 2 * LANES].astype(v_ref.dtype)


def _kvup_proj(dm, ckv, w, kr, n_heads):
    tm = dm.tm
    kv_rank = ckv.shape[1]
    hb = math.gcd(4, n_heads)
    return pl.pallas_call(
        functools.partial(_kvup_kernel, heads=hb),
        grid=(dm.t_all // tm, n_heads // hb),
        in_specs=[
            pl.BlockSpec((tm, kv_rank), lambda i, j: (i, 0)),
            pl.BlockSpec((kv_rank, hb * 2 * LANES), lambda i, j: (0, j)),
            pl.BlockSpec((tm, LANES), lambda i, j: (i, 0)),
        ],
        out_specs=[
            pl.BlockSpec((tm, hb * 2 * LANES), lambda i, j: (i, j)),
            pl.BlockSpec((tm, hb * LANES), lambda i, j: (i, j)),
        ],
        out_shape=[
            jax.ShapeDtypeStruct((dm.t_all, n_heads * 2 * LANES), BF16),
            jax.ShapeDtypeStruct((dm.t_all, n_heads * LANES), BF16),
        ],
        compiler_params=_cparams("arbitrary", "arbitrary"),
        name="mla_kv_up",
    )(ckv, w, kr)


def _mla_flash_kernel(q_ref, kl_ref, kc_ref, vl_ref, vc_ref, o_ref, m_sc, l_sc, acc_sc, *, tk, n_chunks):
    q = q_ref[...]
    m_sc[...] = jnp.full(m_sc.shape, NEG_INF, F32)
    l_sc[...] = jnp.zeros(l_sc.shape, F32)
    acc_sc[...] = jnp.zeros(acc_sc.shape, F32)

    def step(k, v):
        s = _dot_nt(q, k)
        m_prev = m_sc[...]
        m_new = jnp.maximum(m_prev, jnp.max(s, axis=1, keepdims=True))
        alpha = jnp.exp2(m_prev - m_new)
        p = jnp.exp2(s - jnp.concatenate([m_new] * (s.shape[1] // LANES), axis=1))
        l_sc[...] = alpha * l_sc[...] + jnp.sum(p, axis=1, keepdims=True)
        acc_sc[...] = alpha * acc_sc[...] + _dot(p.astype(BF16), v)
        m_sc[...] = m_new

    def body(ci, carry):
        start = pl.multiple_of(ci * tk, tk)
        step(kl_ref[pl.ds(start, tk), :], vl_ref[pl.ds(start, tk), :])
        return carry

    lax.fori_loop(0, n_chunks, body, 0)
    step(kc_ref[...], vc_ref[...])
    o_ref[...] = (acc_sc[...] / l_sc[...]).astype(o_ref.dtype)


def _mla_flash(dm, q, k, v, n_heads):
    tq = min(512, dm.s)
    tk = min(512, dm.s)
    nq = dm.s // tq
    return pl.pallas_call(
        functools.partial(_mla_flash_kernel, tk=tk, n_chunks=dm.s // tk),
        grid=(dm.b, n_heads, nq),
        in_specs=[
            pl.BlockSpec((tq, 2 * LANES), lambda b, h, i: (b * nq + i, h)),
            pl.BlockSpec((dm.s, 2 * LANES), lambda b, h, i: (b, h)),
            pl.BlockSpec((dm.c, 2 * LANES), lambda b, h, i: (dm.t_lat // dm.c + b, h)),
            pl.BlockSpec((dm.s, LANES), lambda b, h, i: (b, h)),
            pl.BlockSpec((dm.c, LANES), lambda b, h, i: (dm.t_lat // dm.c + b, h)),
        ],
        out_specs=pl.BlockSpec((tq, LANES), lambda b, h, i: (b * nq + i, h)),
        out_shape=jax.ShapeDtypeStruct((dm.t_lat, n_heads * LANES), BF16),
        scratch_shapes=[
            pltpu.VMEM((tq, LANES), F32),
            pltpu.VMEM((tq, LANES), F32),
            pltpu.VMEM((tq, LANES), F32),
        ],
        compiler_params=_cparams("arbitrary", "arbitrary", "arbitrary"),
        name="mla_flash",
    )(q, k, k, v, v)


def _mixer_a(dm, h, w_qkv, sink, cos_t, sin_t):
    n_heads = sink.shape[0]
    n_kv = (w_qkv.shape[1] // A_HEAD_DIM - n_heads) // 2
    assert (n_heads // n_kv) % 2 == 0 and n_kv % 2 == 0
    z = _qkv_proj(dm, h, w_qkv.astype(BF16), cos_t, sin_t, n_heads * A_HEAD_DIM, n_kv * A_HEAD_DIM)
    return _attn_a(dm, z, sink, n_heads, n_kv)


def _mixer_b(dm, h, w_dkv, g_q, g_kv, w_uq, w_ukv, wo_rows, cos_t, sin_t):
    q_rank, kv_rank = g_q.shape[0], g_kv.shape[0]
    rope = w_dkv.shape[1] - q_rank - kv_rank
    n_heads = (w_uq.shape[1] - w_ukv.shape[1] + wo_rows) // rope
    nope = (w_ukv.shape[1] - wo_rows) // n_heads
    d_v = wo_rows // n_heads
    assert rope == A_HEAD_DIM and nope == LANES and d_v == LANES
    wq = w_dkv[:, :q_rank].astype(BF16)
    wkv = jnp.pad(w_dkv[:, q_rank:], ((0, 0), (0, LANES - rope))).astype(BF16)
    zq, ckv, kr = _dkv_proj(dm, h, wq, wkv, g_q, g_kv, cos_t, sin_t)
    w_uq_p = jnp.pad(w_uq.reshape(q_rank, n_heads, nope + rope), ((0, 0), (0, 0), (0, LANES - rope)))
    w_uq_p = w_uq_p.reshape(q_rank, n_heads * 2 * LANES).astype(BF16)
    qscale = (nope + rope) ** -0.5 * math.log2(math.e)
    q = _qup_proj(dm, zq, w_uq_p, cos_t, sin_t, qscale)
    k, v = _kvup_proj(dm, ckv, w_ukv.astype(BF16), kr, n_heads)
    return _mla_flash(dm, q, k, v, n_heads)


def kernel(x, c, ctx, c_ctx, ada_w, ada_b, norm_g, final_g, a_wqkv, a_wo, a_sink, b_wdkv, b_gq, b_gkv, b_wuq,
           b_wukv, b_wo, r_wg, r_bg, r_we, r_be, e_wgu, e_wdn):
    b, s, d = x.shape
    c_len = ctx.shape[1]
    depth = ada_w.shape[0]
    dm = _Dims(b, s, c_len, d)
    assert b + 1 <= MOD_ROWS and dm.t_lat % c_len == 0 and c_len % WINDOW == 0 and s % GRID_W == 0
    n_groups, n_experts = r_wg.shape[2], r_we.shape[2]
    assert n_groups + n_experts <= LANES

    xa = jnp.concatenate([x.reshape(dm.t_lat, d), ctx.reshape(dm.t_ctx, d)], axis=0)
    cc = jnp.concatenate([c, c_ctx[None, :], jnp.zeros((MOD_ROWS - b - 1, d), F32)], axis=0)
    mod = _ada(cc, ada_w, ada_b)
    mod3 = [mod[i].reshape(MOD_ROWS, 1, 6 * d) for i in range(depth)]
    cos_t, sin_t = _rope_tables(s, A_HEAD_DIM, dm.tm)

    h = _modulate(dm, xa, norm_g[0, 0], mod3[0], 0, 1)
    out = None
    for i in range(depth):
        last = i == depth - 1
        j = i // 2
        rows = dm.t_lat if last else dm.t_all
        if i % 2 == 0:
            o = _mixer_a(dm, h, a_wqkv[j], a_sink[j], cos_t, sin_t)
            w_o = a_wo[j]
        else:
            o = _mixer_b(dm, h, b_wdkv[j], b_gq[j], b_gkv[j], b_wuq[j], b_wukv[j], b_wo.shape[1], cos_t, sin_t)
            w_o = b_wo[j]
        x1 = _mm_res(dm, o, w_o.astype(BF16), xa, mod3[i], 2, rows)
        w_r = jnp.pad(jnp.concatenate([r_wg[i], r_we[i]], axis=1), ((0, 0), (0, LANES - n_groups - n_experts)))
        b_r = jnp.pad(jnp.concatenate([r_bg[i], r_be[i]]), (0, LANES - n_groups - n_experts)).reshape(1, LANES)
        h2, route = _mod_router(dm, x1, norm_g[i, 1], mod3[i], 3, 4, w_r, b_r, rows, n_groups, n_experts)
        blk_e, src_tok, n_used, dest, n_rows = _dispatch(route, n_experts)
        y_buf = _moe_experts(h2, blk_e, src_tok, n_used, e_wgu[i].astype(BF16), e_wdn[i].astype(BF16), n_rows)
        if last:
            out = _combine(dm, y_buf, dest, x1, route, mod3[i], 5, rows, final_g=final_g)
        else:
            xa, h = _combine(dm, y_buf, dest, x1, route, mod3[i], 5, rows,
                             next_g=norm_g[i + 1, 0], next_mod3=mod3[i + 1])
    return out.reshape(b, s, d)
```

```python
import functools
import math

import jax
import jax.numpy as jnp
from jax import lax
from jax.experimental import pallas as pl
from jax.experimental.pallas import tpu as pltpu

F32 = jnp.float32
BF16 = jnp.bfloat16

EPS = 1e-6
NEG_INF = -1e30
ROPE_BASE = 10000.0
GRID_W = 64
A_HEAD_DIM = 64
WINDOW = 128
MOE_BLOCK = 128
LANES = 128
MOD_ROWS = 8
VMEM_LIMIT_BYTES = 56 * 1024 * 1024


def _cparams(*sem):
    return pltpu.CompilerParams(dimension_semantics=sem, vmem_limit_bytes=VMEM_LIMIT_BYTES)


def _dot(a, b):
    return jnp.dot(a, b, preferred_element_type=F32)


def _dot_nt(a, b):
    return lax.dot_general(a, b, (((1,), (1,)), ((), ())), preferred_element_type=F32)


def _split_bf16(v):
    hi = v.astype(BF16)
    lo = (v - hi.astype(F32)).astype(BF16)
    return hi, lo


def _rms(x):
    return x * lax.rsqrt(jnp.mean(x * x, axis=-1, keepdims=True) + EPS)


def _modulate_val(x, g, shift, scale):
    return _rms(x) * g * (1.0 + scale) + shift


def _rope128(x, cos, sin_signed):
    lane = lax.broadcasted_iota(jnp.int32, x.shape, 1)
    first_half = (lane & 32) == 0
    partner = jnp.where(first_half, pltpu.roll(x, 96, 1), pltpu.roll(x, 32, 1))
    return x * cos + partner * sin_signed


def _ada_kernel(c_ref, w_ref, b_ref, o_ref):
    c = c_ref[...]
    s = c * jax.nn.sigmoid(c)
    s_hi, s_lo = _split_bf16(s)
    w_hi, w_lo = _split_bf16(w_ref[...])
    r = _dot(jnp.concatenate([s_hi, s_lo], axis=0), w_hi)
    o_ref[...] = r[:MOD_ROWS] + r[MOD_ROWS:] + _dot(s_hi, w_lo) + b_ref[...]


def _ada(cc, ada_w, ada_b):
    depth, d, n = ada_w.shape
    tn = 512
    return pl.pallas_call(
        _ada_kernel,
        grid=(depth, n // tn),
        in_specs=[
            pl.BlockSpec((MOD_ROWS, d), lambda l, j: (0, 0)),
            pl.BlockSpec((None, d, tn), lambda l, j: (l, 0, j)),
            pl.BlockSpec((None, 1, tn), lambda l, j: (l, 0, j)),
        ],
        out_specs=pl.BlockSpec((None, MOD_ROWS, tn), lambda l, j: (l, 0, j)),
        out_shape=jax.ShapeDtypeStruct((depth, MOD_ROWS, n), F32),
        compiler_params=_cparams("arbitrary", "arbitrary"),
        name="ada",
    )(cc, ada_w, ada_b.reshape(depth, 1, n))


class _Dims:
    def __init__(self, b, s, c, d):
        self.b, self.s, self.c, self.d = b, s, c, d
        self.t_lat, self.t_ctx = b * s, b * c
        self.t_all = self.t_lat + self.t_ctx
        self.tm = next(t for t in (512, 256, 128) if s % t == 0 and self.t_ctx % t == 0)

    def modrow(self, i, tm):
        return jnp.minimum(i // (self.s // tm), self.b)

    def ropeblk(self, i, tm):
        return jnp.where(i < self.t_lat // tm, i % (self.s // tm), self.s // tm)

    def mod_spec(self, col, tm, width=None):
        width = self.d if width is None else width
        per = self.d // width
        if width == self.d:
            return pl.BlockSpec((1, 1, width), lambda i, *_: (self.modrow(i, tm), 0, col))
        return pl.BlockSpec((1, 1, width), lambda i, j, *_: (self.modrow(i, tm), 0, col * per + j))


def _rope_tables(s, d_rot, pad_rows):
    rows = s // GRID_W
    row = jnp.repeat(jnp.arange(rows, dtype=F32), GRID_W)
    col = jnp.tile(jnp.arange(GRID_W, dtype=F32), rows)
    n_freq = d_rot // 4
    inv = ROPE_BASE ** (-jnp.arange(n_freq, dtype=F32) / n_freq)
    ang = jnp.concatenate([row[:, None] * inv, col[:, None] * inv], axis=-1)
    cos, sin = jnp.cos(ang), jnp.sin(ang)
    reps = LANES // d_rot
    cos_t = jnp.tile(cos, (1, 2 * reps))
    sin_t = jnp.tile(jnp.concatenate([-sin, sin], axis=-1), (1, reps))
    cos_t = jnp.concatenate([cos_t, jnp.ones((pad_rows, LANES), F32)], axis=0)
    sin_t = jnp.concatenate([sin_t, jnp.zeros((pad_rows, LANES), F32)], axis=0)
    return cos_t, sin_t


def _modulate_kernel(x_ref, g_ref, sh_ref, sc_ref, o_ref):
    o_ref[...] = _modulate_val(x_ref[...], g_ref[...], sh_ref[0], sc_ref[0]).astype(o_ref.dtype)


def _modulate(dm, x, g, mod3, shift_col, scale_col):
    tm = min(dm.tm, 256)
    return pl.pallas_call(
        _modulate_kernel,
        grid=(dm.t_all // tm,),
        in_specs=[
            pl.BlockSpec((tm, dm.d), lambda i: (i, 0)),
            pl.BlockSpec((1, dm.d), lambda i: (0, 0)),
            dm.mod_spec(shift_col, tm),
            dm.mod_spec(scale_col, tm),
        ],
        out_specs=pl.BlockSpec((tm, dm.d), lambda i: (i, 0)),
        out_shape=jax.ShapeDtypeStruct((dm.t_all, dm.d), BF16),
        compiler_params=_cparams("arbitrary"),
        name="modulate",
    )(x, g.reshape(1, dm.d), mod3, mod3)


def _qkv_kernel(a_ref, w_ref, cos_ref, sin_ref, o_ref, *, n_q_blocks, n_rope_blocks, qscale):
    j = pl.program_id(1)
    acc = _dot(a_ref[...], w_ref[...])

    @pl.when(j >= n_rope_blocks)
    def _():
        o_ref[...] = acc.astype(o_ref.dtype)

    @pl.when(j < n_rope_blocks)
    def _():
        scale = jnp.where(j < n_q_blocks, qscale, 1.0).astype(F32)
        cos, sin = cos_ref[...], sin_ref[...]
        for g in range(acc.shape[1] // LANES):
            sl = slice(g * LANES, (g + 1) * LANES)
            o_ref[:, sl] = (_rope128(acc[:, sl], cos, sin) * scale).astype(o_ref.dtype)


def _qkv_proj(dm, h, w, cos_t, sin_t, qd, kvd):
    tm = dm.tm
    n = w.shape[1]
    tn = math.gcd(512, kvd)
    kern = functools.partial(_qkv_kernel, n_q_blocks=qd // tn, n_rope_blocks=(qd + kvd) // tn,
                             qscale=A_HEAD_DIM ** -0.5)
    return pl.pallas_call(
        kern,
        grid=(dm.t_all // tm, n // tn),
        in_specs=[
            pl.BlockSpec((tm, dm.d), lambda i, j: (i, 0)),
            pl.BlockSpec((dm.d, tn), lambda i, j: (0, j)),
            pl.BlockSpec((tm, LANES), lambda i, j: (dm.ropeblk(i, tm), 0)),
            pl.BlockSpec((tm, LANES), lambda i, j: (dm.ropeblk(i, tm), 0)),
        ],
        out_specs=pl.BlockSpec((tm, tn), lambda i, j: (i, j)),
        out_shape=jax.ShapeDtypeStruct((dm.t_all, n), BF16),
        compiler_params=_cparams("arbitrary", "arbitrary"),
        name="qkv_proj",
    )(h, w, cos_t, sin_t)


def _attn_a_kernel(sink_ref, q_ref, kp_ref, kc_ref, kn_ref, vp_ref, vc_ref, vn_ref, kx_ref, vx_ref, o_ref,
                   *, nb, group):
    i = pl.program_id(1)
    hp = pl.program_id(2)
    npair = group // 2
    w = WINDOW
    lane = lax.broadcasted_iota(jnp.int32, (1, LANES), 1)
    lo_mask = lane < A_HEAD_DIM

    k_loc = jnp.concatenate([kp_ref[...], kc_ref[...], kn_ref[...]], axis=0).astype(F32)
    v_loc = jnp.concatenate([vp_ref[...], vc_ref[...], vn_ref[...]], axis=0).astype(F32)
    k_ctx = kx_ref[...].astype(F32)
    v_ctx = vx_ref[...].astype(F32)

    r = lax.broadcasted_iota(jnp.int32, (w, 3 * w), 0)
    c = lax.broadcasted_iota(jnp.int32, (w, 3 * w), 1)
    rel = c - w - r
    valid = ((rel >= -w) & (rel <= w) & ((c >= w) | (i > 0)) & ((c < 2 * w) | (i < nb - 1)) & (i < nb))
    bias = jnp.where(valid, 0.0, NEG_INF).astype(F32)

    for par in range(2):
        def split(t):
            if par == 0:
                left = jnp.where(lo_mask, t, 0.0)
                right = pltpu.roll(left, A_HEAD_DIM, 1)
            else:
                right = jnp.where(lo_mask, 0.0, t)
                left = pltpu.roll(right, A_HEAD_DIM, 1)
            return left.astype(BF16), right.astype(BF16)

        k_loc_lr, k_ctx_lr = split(k_loc), split(k_ctx)
        v_loc_lr, v_ctx_lr = split(v_loc), split(v_ctx)
        base = par * group * A_HEAD_DIM
        qs = jnp.concatenate([q_ref[:, base + pp * LANES: base + (pp + 1) * LANES] for pp in range(npair)], axis=0)

        o_pair = None
        for side in range(2):
            s_loc = _dot_nt(qs, k_loc_lr[side])
            s_ctx = _dot_nt(qs, k_ctx_lr[side])
            e_loc, e_ctx, dens = [], [], []
            for pp in range(npair):
                sink = sink_ref[(2 * hp + par) * group + 2 * pp + side]
                sl = s_loc[pp * w:(pp + 1) * w] + bias
                sc = s_ctx[pp * w:(pp + 1) * w]
                m = jnp.maximum(jnp.maximum(jnp.max(sl, axis=1, keepdims=True),
                                            jnp.max(sc, axis=1, keepdims=True)), sink)
                el = jnp.exp(sl - m)
                ec = jnp.exp(sc - m)
                dens.append(jnp.sum(el, axis=1, keepdims=True) + jnp.sum(ec, axis=1, keepdims=True)
                            + jnp.exp(sink - m))
                e_loc.append(el.astype(BF16))
                e_ctx.append(ec.astype(BF16))
            o = _dot(jnp.concatenate(e_loc, axis=0), v_loc_lr[side]) + _dot(jnp.concatenate(e_ctx, axis=0),
                                                                              v_ctx_lr[side])
            o = o / jnp.concatenate(dens, axis=0)
            o_pair = o if o_pair is None else o_pair + o
        for pp in range(npair):
            o_ref[:, base + pp * LANES: base + (pp + 1) * LANES] = o_pair[pp * w:(pp + 1) * w].astype(o_ref.dtype)


def _attn_a(dm, z, sink, n_heads, n_kv):
    group = n_heads // n_kv
    qd, kvd = n_heads * A_HEAD_DIM, n_kv * A_HEAD_DIM
    w = WINDOW
    nb = dm.s // w
    ncb = dm.c // w
    lat_blocks = dm.t_lat // w
    qw = 2 * group * A_HEAD_DIM
    kcol, vcol = qd // LANES, (qd + kvd) // LANES

    def qrow(b, i):
        return jnp.where(i < nb, b * nb + i, lat_blocks + b * ncb + (i - nb))

    def krow(b, i, off):
        return b * nb + jnp.clip(jnp.minimum(i, nb - 1) + off, 0, nb - 1)

    def kv_spec(col, off):
        return pl.BlockSpec((w, LANES), lambda b, i, hp, s_: (krow(b, i, off), col + hp))

    def ctx_spec(col):
        return pl.BlockSpec((dm.c, LANES), lambda b, i, hp, s_: (dm.t_lat // dm.c + b, col + hp))

    grid_spec = pltpu.PrefetchScalarGridSpec(
        num_scalar_prefetch=1,
        grid=(dm.b, nb + ncb, n_kv // 2),
        in_specs=[
            pl.BlockSpec((w, qw), lambda b, i, hp, s_: (qrow(b, i), hp)),
            kv_spec(kcol, -1), kv_spec(kcol, 0), kv_spec(kcol, 1),
            kv_spec(vcol, -1), kv_spec(vcol, 0), kv_spec(vcol, 1),
            ctx_spec(kcol), ctx_spec(vcol),
        ],
        out_specs=pl.BlockSpec((w, qw), lambda b, i, hp, s_: (qrow(b, i), hp)),
    )
    return pl.pallas_call(
        functools.partial(_attn_a_kernel, nb=nb, group=group),
        grid_spec=grid_spec,
        out_shape=jax.ShapeDtypeStruct((dm.t_all, qd), BF16),
        compiler_params=_cparams("arbitrary", "arbitrary", "arbitrary"),
        name="attn_window",
    )(sink, z, z, z, z, z, z, z, z, z)


def _mm_res_kernel(a_ref, w_ref, res_ref, gate_ref, o_ref):
    o_ref[...] = res_ref[...] + gate_ref[0] * _dot(a_ref[...], w_ref[...])


def _mm_res(dm, a, w, res, mod3, gate_col, rows):
    tm = dm.tm
    k, n = w.shape
    tn = min(512, n)
    return pl.pallas_call(
        _mm_res_kernel,
        grid=(rows // tm, n // tn),
        in_specs=[
            pl.BlockSpec((tm, k), lambda i, j: (i, 0)),
            pl.BlockSpec((k, tn), lambda i, j: (0, j)),
            pl.BlockSpec((tm, tn), lambda i, j: (i, j)),
            dm.mod_spec(gate_col, tm, tn),
        ],
        out_specs=pl.BlockSpec((tm, tn), lambda i, j: (i, j)),
        out_shape=jax.ShapeDtypeStruct((rows, n), F32),
        compiler_params=_cparams("arbitrary", "arbitrary"),
        name="out_proj_residual",
    )(a, w, res, mod3)


def _mod_router_kernel(x_ref, g_ref, sh_ref, sc_ref, wh_ref, wl_ref, br_ref, h_ref, route_ref, *, n_groups, n_experts):
    h = _modulate_val(x_ref[...], g_ref[...], sh_ref[0], sc_ref[0])
    h_ref[...] = h
    h_hi, h_lo = _split_bf16(h)
    wh = wh_ref[...]
    lg = _dot(h_hi, wh) + _dot(h_lo, wh) + _dot(h_hi, wl_ref[...]) + br_ref[...]

    per = n_experts // n_groups
    lane = lax.broadcasted_iota(jnp.int32, lg.shape, 1)
    lane_f = lane.astype(F32)
    big = float(LANES)

    def first_argmax(v, vmax):
        return jnp.min(jnp.where(v == vmax, lane_f, big), axis=1, keepdims=True)

    is_g = lane < n_groups
    g_l = jnp.where(is_g, lg, NEG_INF)
    gmax = jnp.max(g_l, axis=1, keepdims=True)
    grp = first_argmax(g_l, gmax)
    p_grp = 1.0 / jnp.sum(jnp.where(is_g, jnp.exp(g_l - gmax), 0.0), axis=1, keepdims=True)

    lo = n_groups + grp * per
    e_l = jnp.where((lane_f >= lo) & (lane_f < lo + per), lg, NEG_INF)
    t1 = jnp.max(e_l, axis=1, keepdims=True)
    i1 = first_argmax(e_l, t1)
    e_l2 = jnp.where(lane_f == i1, NEG_INF, e_l)
    t2 = jnp.max(e_l2, axis=1, keepdims=True)
    i2 = first_argmax(e_l2, t2)
    ratio = jnp.exp(t2 - t1)
    gate1 = p_grp / (1.0 + ratio)
    gate2 = gate1 * ratio
    route = jnp.where(lane == 0, i1 - n_groups,
                      jnp.where(lane == 1, i2 - n_groups,
                                jnp.where(lane == 2, gate1, jnp.where(lane == 3, gate2, 0.0))))
    route_ref[...] = route


def _mod_router(dm, x, g, mod3, shift_col, scale_col, w_r, b_r, rows, n_groups, n_experts):
    tm = min(dm.tm, 256)
    w_hi, w_lo = _split_bf16(w_r)
    kern = functools.partial(_mod_router_kernel, n_groups=n_groups, n_experts=n_experts)
    return pl.pallas_call(
        kern,
        grid=(rows // tm,),
        in_specs=[
            pl.BlockSpec((tm, dm.d), lambda i: (i, 0)),
            pl.BlockSpec((1, dm.d), lambda i: (0, 0)),
            dm.mod_spec(shift_col, tm),
            dm.mod_spec(scale_col, tm),
            pl.BlockSpec((dm.d, LANES), lambda i: (0, 0)),
            pl.BlockSpec((dm.d, LANES), lambda i: (0, 0)),
            pl.BlockSpec((1, LANES), lambda i: (0, 0)),
        ],
        out_specs=[
            pl.BlockSpec((tm, dm.d), lambda i: (i, 0)),
            pl.BlockSpec((tm, LANES), lambda i: (i, 0)),
        ],
        out_shape=[
            jax.ShapeDtypeStruct((rows, dm.d), F32),
            jax.ShapeDtypeStruct((rows, LANES), F32),
        ],
        compiler_params=_cparams("arbitrary"),
        name="modulate_router",
    )(x, g.reshape(1, dm.d), mod3, mod3, w_hi, w_lo, b_r)


def _dispatch(route, n_experts):
    eid = route[:, :2].astype(jnp.int32).reshape(-1)
    n_assign = eid.shape[0]
    order = jnp.argsort(eid).astype(jnp.int32)
    rank = jnp.argsort(order).astype(jnp.int32)
    onehot = eid[:, None] == jnp.arange(n_experts, dtype=jnp.int32)[None, :]
    counts = jnp.sum(onehot, axis=0, dtype=jnp.int32)
    padded = (counts + MOE_BLOCK - 1) // MOE_BLOCK * MOE_BLOCK
    p_end = jnp.cumsum(padded)
    p_start = p_end - padded
    c_start = jnp.cumsum(counts) - counts
    dest = rank + jnp.sum(jnp.where(onehot, (p_start - c_start)[None, :], 0), axis=1, dtype=jnp.int32)
    n_rows = (n_assign + n_experts * (MOE_BLOCK - 1) + MOE_BLOCK - 1) // MOE_BLOCK * MOE_BLOCK
    n_blocks = n_rows // MOE_BLOCK
    blk_start = jnp.arange(n_blocks, dtype=jnp.int32) * MOE_BLOCK
    blk_e = jnp.minimum(jnp.sum(p_end[None, :] <= blk_start[:, None], axis=1, dtype=jnp.int32), n_experts - 1)
    off = (blk_start - p_start[blk_e])[:, None] + jnp.arange(MOE_BLOCK, dtype=jnp.int32)[None, :]
    pos = jnp.clip(c_start[blk_e][:, None] + off, 0, n_assign - 1)
    src_tok = jnp.where(off < counts[blk_e][:, None], order[pos] // 2, 0).reshape(n_rows).astype(jnp.int32)
    n_used = (p_end[-1:] // MOE_BLOCK).astype(jnp.int32)
    return blk_e, src_tok, n_used, dest, n_rows


def _moe_kernel(blk_e_ref, src_ref, nused_ref, h_hbm, wgu_ref, wdn_ref, y_ref, xb, sem, *, d_exp):
    del blk_e_ref
    b = pl.program_id(0)
    n_used = nused_ref[0]
    slot = b % 2

    def row_copies(blk, slot_):
        return [pltpu.make_async_copy(h_hbm.at[pl.ds(src_ref[blk * MOE_BLOCK + r], 1), :],
                                      xb.at[slot_, pl.ds(r, 1), :], sem.at[slot_])
                for r in range(MOE_BLOCK)]

    @pl.when(b == 0)
    def _():
        for cp in row_copies(0, 0):
            cp.start()

    @pl.when(b >= n_used)
    def _():
        y_ref[...] = jnp.zeros(y_ref.shape, y_ref.dtype)

    @pl.when(b < n_used)
    def _():
        for cp in row_copies(b, slot):
            cp.wait()

        @pl.when(b + 1 < n_used)
        def _():
            for cp in row_copies(b + 1, 1 - slot):
                cp.start()

        xrow = xb[slot].astype(BF16)
        gu = _dot(xrow, wgu_ref[...])
        g, u = gu[:, :d_exp], gu[:, d_exp:]
        act = (g * jax.nn.sigmoid(g) * u).astype(BF16)
        y_ref[...] = _dot(act, wdn_ref[...])


def _moe_experts(h, blk_e, src_tok, n_used, w_gu, w_dn, layer, n_rows):
    _, _, d, f2 = w_gu.shape
    d_exp = f2 // 2
    n_blocks = n_rows // MOE_BLOCK
    grid_spec = pltpu.PrefetchScalarGridSpec(
        num_scalar_prefetch=3,
        grid=(n_blocks,),
        in_specs=[
            pl.BlockSpec(memory_space=pl.ANY),
            pl.BlockSpec((None, None, d, f2), lambda b, be, st, nu: (layer, be[b], 0, 0)),
            pl.BlockSpec((None, None, d_exp, d), lambda b, be, st, nu: (layer, be[b], 0, 0)),
        ],
        out_specs=pl.BlockSpec((MOE_BLOCK, d), lambda b, be, st, nu: (b, 0)),
        scratch_shapes=[
            pltpu.VMEM((2, MOE_BLOCK, d), F32),
            pltpu.SemaphoreType.DMA((2,)),
        ],
    )
    return pl.pallas_call(
        functools.partial(_moe_kernel, d_exp=d_exp),
        grid_spec=grid_spec,
        out_shape=jax.ShapeDtypeStruct((n_rows, d), F32),
        compiler_params=_cparams("arbitrary"),
        name="moe_experts",
    )(blk_e, src_tok, n_used, h, w_gu, w_dn)


def _combine_kernel(dest_ref, y_hbm, x_ref, route_ref, g2_ref, *refs, tc, n_blocks, final):
    if final:
        fg_ref, out_ref, yb, sem = refs
    else:
        ng_ref, sh_ref, sc_ref, xo_ref, h_ref, yb, sem = refs
    t = pl.program_id(0)
    slot = t % 2

    def row_copies(blk, slot_):
        return [pltpu.make_async_copy(y_hbm.at[pl.ds(dest_ref[(blk * tc + r) * 2 + k], 1), :],
                                      yb.at[slot_, k, pl.ds(r, 1), :], sem.at[slot_])
                for r in range(tc) for k in range(2)]

    @pl.when(t == 0)
    def _():
        for cp in row_copies(0, 0):
            cp.start()

    for cp in row_copies(t, slot):
        cp.wait()

    @pl.when(t + 1 < n_blocks)
    def _():
        for cp in row_copies(t + 1, 1 - slot):
            cp.start()

    route = route_ref[...]
    y = route[:, 2:3] * yb[slot, 0] + route[:, 3:4] * yb[slot, 1]
    xn = x_ref[...] + g2_ref[0] * y
    if final:
        out_ref[...] = _rms(xn) * fg_ref[...]
    else:
        xo_ref[...] = xn
        h_ref[...] = _modulate_val(xn, ng_ref[...], sh_ref[0], sc_ref[0]).astype(h_ref.dtype)


def _combine(dm, y_buf, dest, x, route, mod3, gate_col, rows, *, final_g=None, next_g=None, next_mod3=None):
    tc = MOE_BLOCK
    n_blocks = rows // tc
    final = final_g is not None
    row_spec = pl.BlockSpec((tc, dm.d), lambda t, d_: (t, 0))
    vec_spec = pl.BlockSpec((1, dm.d), lambda t, d_: (0, 0))
    in_specs = [
        pl.BlockSpec(memory_space=pl.ANY),
        row_spec,
        pl.BlockSpec((tc, LANES), lambda t, d_: (t, 0)),
        dm.mod_spec(gate_col, tc),
    ]
    if final:
        in_specs += [vec_spec]
        args = (final_g.reshape(1, dm.d),)
        out_specs = row_spec
        out_shape = jax.ShapeDtypeStruct((rows, dm.d), F32)
    else:
        in_specs += [vec_spec, dm.mod_spec(0, tc), dm.mod_spec(1, tc)]
        args = (next_g.reshape(1, dm.d), next_mod3, next_mod3)
        out_specs = [row_spec, row_spec]
        out_shape = [jax.ShapeDtypeStruct((rows, dm.d), F32), jax.ShapeDtypeStruct((rows, dm.d), BF16)]
    grid_spec = pltpu.PrefetchScalarGridSpec(
        num_scalar_prefetch=1,
        grid=(n_blocks,),
        in_specs=in_specs,
        out_specs=out_specs,
        scratch_shapes=[
            pltpu.VMEM((2, 2, tc, dm.d), F32),
            pltpu.SemaphoreType.DMA((2,)),
        ],
    )
    return pl.pallas_call(
        functools.partial(_combine_kernel, tc=tc, n_blocks=n_blocks, final=final),
        grid_spec=grid_spec,
        out_shape=out_shape,
        compiler_params=_cparams("arbitrary"),
        name="moe_combine",
    )(dest, y_buf, x, route, mod3, *args)


def _dkv_kernel(a_ref, wq_ref, wkv_ref, gq_ref, gkv_ref, cos_ref, sin_ref, zq_ref, ckv_ref, kr_ref, *, kv_rank):
    a = a_ref[...]
    zq_ref[...] = (_rms(_dot(a, wq_ref[...])) * gq_ref[...]).astype(zq_ref.dtype)
    zkv = _dot(a, wkv_ref[...])
    ckv_ref[...] = (_rms(zkv[:, :kv_rank]) * gkv_ref[...]).astype(ckv_ref.dtype)
    kr_ref[...] = _rope128(zkv[:, kv_rank:], cos_ref[...], sin_ref[...]).astype(kr_ref.dtype)


def _dkv_proj(dm, h, wq, wkv, gq, gkv, cos_t, sin_t):
    tm = dm.tm
    q_rank = wq.shape[1]
    kv_rank = wkv.shape[1] - LANES
    rope_spec = pl.BlockSpec((tm, LANES), lambda i: (dm.ropeblk(i, tm), 0))
    return pl.pallas_call(
        functools.partial(_dkv_kernel, kv_rank=kv_rank),
        grid=(dm.t_all // tm,),
        in_specs=[
            pl.BlockSpec((tm, dm.d), lambda i: (i, 0)),
            pl.BlockSpec((dm.d, q_rank), lambda i: (0, 0)),
            pl.BlockSpec((dm.d, kv_rank + LANES), lambda i: (0, 0)),
            pl.BlockSpec((1, q_rank), lambda i: (0, 0)),
            pl.BlockSpec((1, kv_rank), lambda i: (0, 0)),
            rope_spec, rope_spec,
        ],
        out_specs=[
            pl.BlockSpec((tm, q_rank), lambda i: (i, 0)),
            pl.BlockSpec((tm, kv_rank), lambda i: (i, 0)),
            pl.BlockSpec((tm, LANES), lambda i: (i, 0)),
        ],
        out_shape=[
            jax.ShapeDtypeStruct((dm.t_all, q_rank), BF16),
            jax.ShapeDtypeStruct((dm.t_all, kv_rank), BF16),
            jax.ShapeDtypeStruct((dm.t_all, LANES), BF16),
        ],
        compiler_params=_cparams("arbitrary"),
        name="mla_down_proj",
    )(h, wq, wkv, gq.reshape(1, -1), gkv.reshape(1, -1), cos_t, sin_t)


def _qup_kernel(a_ref, w_ref, cos_ref, sin_ref, o_ref, *, qscale):
    acc = _dot(a_ref[...], w_ref[...])
    cos, sin = cos_ref[...], sin_ref[...]
    for g in range(acc.shape[1] // LANES):
        sl = slice(g * LANES, (g + 1) * LANES)
        x = acc[:, sl]
        if g % 2 == 1:
            x = _rope128(x, cos, sin)
        o_ref[:, sl] = (x * qscale).astype(o_ref.dtype)


def _qup_proj(dm, zq, w, cos_t, sin_t, qscale):
    tm = dm.tm
    k, n = w.shape
    tn = min(1024, n)
    rope_spec = pl.BlockSpec((tm, LANES), lambda i, j: (dm.ropeblk(i, tm), 0))
    return pl.pallas_call(
        functools.partial(_qup_kernel, qscale=qscale),
        grid=(dm.t_lat // tm, n // tn),
        in_specs=[
            pl.BlockSpec((tm, k), lambda i, j: (i, 0)),
            pl.BlockSpec((k, tn), lambda i, j: (0, j)),
            rope_spec, rope_spec,
        ],
        out_specs=pl.BlockSpec((tm, tn), lambda i, j: (i, j)),
        out_shape=jax.ShapeDtypeStruct((dm.t_lat, n), BF16),
        compiler_params=_cparams("arbitrary", "arbitrary"),
        name="mla_q_up",
    )(zq, w, cos_t, sin_t)


def _kvup_kernel(a_ref, wk_ref, wvt_ref, kr_ref, k_ref, vt_ref, *, heads):
    a = a_ref[...]
    kn = _dot(a, wk_ref[...])
    kr = kr_ref[...]
    for h in range(heads):
        k_ref[:, h * 2 * LANES: h * 2 * LANES + LANES] = kn[:, h * LANES: (h + 1) * LANES].astype(k_ref.dtype)
        k_ref[:, h * 2 * LANES + LANES: (h + 1) * 2 * LANES] = kr
    vt_ref[...] = _dot_nt(wvt_ref[...], a).astype(vt_ref.dtype)


def _kvup_proj(dm, ckv, w_uk, w_uvt, kr, n_heads):
    tm = dm.tm
    kv_rank = ckv.shape[1]
    hb = math.gcd(4, n_heads)
    return pl.pallas_call(
        functools.partial(_kvup_kernel, heads=hb),
        grid=(dm.t_all // tm, n_heads // hb),
        in_specs=[
            pl.BlockSpec((tm, kv_rank), lambda i, j: (i, 0)),
            pl.BlockSpec((kv_rank, hb * LANES), lambda i, j: (0, j)),
            pl.BlockSpec((hb * LANES, kv_rank), lambda i, j: (j, 0)),
            pl.BlockSpec((tm, LANES), lambda i, j: (i, 0)),
        ],
        out_specs=[
            pl.BlockSpec((tm, hb * 2 * LANES), lambda i, j: (i, j)),
            pl.BlockSpec((hb * LANES, tm), lambda i, j: (j, i)),
        ],
        out_shape=[
            jax.ShapeDtypeStruct((dm.t_all, n_heads * 2 * LANES), BF16),
            jax.ShapeDtypeStruct((n_heads * LANES, dm.t_all), BF16),
        ],
        compiler_params=_cparams("arbitrary", "arbitrary"),
        name="mla_kv_up",
    )(ckv, w_uk, w_uvt, kr)


def _mla_flash_kernel(q_ref, kl_ref, kc_ref, vl_ref, vc_ref, o_ref, m_sc, l_sc, acc_sc, sa_sc, sb_sc, pa_sc, pb_sc,
                      qt_sc, *, tk, n_chunks):
    tq = q_ref.shape[0]
    half = tq // 2
    m_sc[...] = jnp.full(m_sc.shape, NEG_INF, F32)
    l_sc[...] = jnp.zeros(l_sc.shape, F32)
    acc_sc[...] = jnp.zeros(acc_sc.shape, F32)
    pb_sc[...] = jnp.zeros(pb_sc.shape, pb_sc.dtype)
    qt_sc[...] = q_ref[...].T

    def k_chunk(ci):
        return kl_ref[pl.ds(pl.multiple_of(ci * tk, tk), tk), :]

    def vt_chunk(ci):
        return vl_ref[:, pl.ds(pl.multiple_of(ci * tk, tk), tk)]

    def softmax(s, sl):
        m_prev = m_sc[:, sl]
        m_new = jnp.maximum(m_prev, jnp.max(s, axis=0, keepdims=True))
        alpha = jnp.exp2(m_prev - m_new)
        p = jnp.exp2(s - m_new)
        l_sc[:, sl] = alpha * l_sc[:, sl] + jnp.sum(p, axis=0, keepdims=True)
        m_sc[:, sl] = m_new
        return alpha, p.astype(BF16)

    def stage(c_prev, c_next, s_cur, s_next, p_prev, p_cur):
        vt = vt_chunk(c_prev)
        kn = k_chunk(c_next)
        for hf in range(2):
            sl = slice(hf * half, (hf + 1) * half)
            pv = _dot(vt, p_prev[:, sl])
            s_next[:, sl] = _dot(kn, qt_sc[:, sl])
            alpha, p = softmax(s_cur[:, sl], sl)
            p_cur[:, sl] = p
            acc_sc[:, sl] = alpha * (acc_sc[:, sl] + pv)

    sa_sc[...] = _dot(k_chunk(0), qt_sc[...])

    def body(j, carry):
        c0 = 2 * j
        stage(jnp.maximum(c0 - 1, 0), c0 + 1, sa_sc, sb_sc, pb_sc, pa_sc)
        stage(c0, jnp.minimum(c0 + 2, n_chunks - 1), sb_sc, sa_sc, pa_sc, pb_sc)
        return carry

    lax.fori_loop(0, n_chunks // 2, body, 0)
    pv = _dot(vt_chunk(n_chunks - 1), pb_sc[...])
    alpha, p = softmax(_dot(kc_ref[...], qt_sc[...]), slice(None))
    acc = alpha * (acc_sc[...] + pv) + _dot(vc_ref[...], p)
    o_ref[...] = (acc / l_sc[...]).T.astype(o_ref.dtype)


def _mla_flash(dm, q, k, vt, n_heads):
    tq = min(512, dm.s)
    tk = min(512, dm.s // 2)
    nq = dm.s // tq
    n_chunks = dm.s // tk
    assert n_chunks % 2 == 0 and tq % (2 * LANES) == 0
    return pl.pallas_call(
        functools.partial(_mla_flash_kernel, tk=tk, n_chunks=n_chunks),
        grid=(dm.b, n_heads, nq),
        in_specs=[
            pl.BlockSpec((tq, 2 * LANES), lambda b, h, i: (b * nq + i, h)),
            pl.BlockSpec((dm.s, 2 * LANES), lambda b, h, i: (b, h)),
            pl.BlockSpec((dm.c, 2 * LANES), lambda b, h, i: (dm.t_lat // dm.c + b, h)),
            pl.BlockSpec((LANES, dm.s), lambda b, h, i: (h, b)),
            pl.BlockSpec((LANES, dm.c), lambda b, h, i: (h, dm.t_lat // dm.c + b)),
        ],
        out_specs=pl.BlockSpec((tq, LANES), lambda b, h, i: (b * nq + i, h)),
        out_shape=jax.ShapeDtypeStruct((dm.t_lat, n_heads * LANES), BF16),
        scratch_shapes=[
            pltpu.VMEM((1, tq), F32),
            pltpu.VMEM((1, tq), F32),
            pltpu.VMEM((LANES, tq), F32),
            pltpu.VMEM((tk, tq), F32),
            pltpu.VMEM((tk, tq), F32),
            pltpu.VMEM((tk, tq), BF16),
            pltpu.VMEM((tk, tq), BF16),
            pltpu.VMEM((2 * LANES, tq), BF16),
        ],
        compiler_params=_cparams("arbitrary", "arbitrary", "arbitrary"),
        name="mla_flash",
    )(q, k, k, vt, vt)


def _mixer_a(dm, h, w_qkv, sink, cos_t, sin_t):
    n_heads = sink.shape[0]
    n_kv = (w_qkv.shape[1] // A_HEAD_DIM - n_heads) // 2
    assert (n_heads // n_kv) % 2 == 0 and n_kv % 2 == 0
    z = _qkv_proj(dm, h, w_qkv.astype(BF16), cos_t, sin_t, n_heads * A_HEAD_DIM, n_kv * A_HEAD_DIM)
    return _attn_a(dm, z, sink, n_heads, n_kv)


def _mixer_b(dm, h, w_dkv, g_q, g_kv, w_uq, w_ukv, wo_rows, cos_t, sin_t):
    q_rank, kv_rank = g_q.shape[0], g_kv.shape[0]
    rope = w_dkv.shape[1] - q_rank - kv_rank
    n_heads = (w_uq.shape[1] - w_ukv.shape[1] + wo_rows) // rope
    nope = (w_ukv.shape[1] - wo_rows) // n_heads
    d_v = wo_rows // n_heads
    assert rope == A_HEAD_DIM and nope == LANES and d_v == LANES
    wq = w_dkv[:, :q_rank].astype(BF16)
    wkv = jnp.pad(w_dkv[:, q_rank:], ((0, 0), (0, LANES - rope))).astype(BF16)
    zq, ckv, kr = _dkv_proj(dm, h, wq, wkv, g_q, g_kv, cos_t, sin_t)
    w_uq_p = jnp.pad(w_uq.reshape(q_rank, n_heads, nope + rope), ((0, 0), (0, 0), (0, LANES - rope)))
    w_uq_p = w_uq_p.reshape(q_rank, n_heads * 2 * LANES).astype(BF16)
    qscale = (nope + rope) ** -0.5 * math.log2(math.e)
    q = _qup_proj(dm, zq, w_uq_p, cos_t, sin_t, qscale)
    w_ukv3 = w_ukv.reshape(kv_rank, n_heads, nope + d_v)
    w_uk = w_ukv3[..., :nope].reshape(kv_rank, n_heads * nope).astype(BF16)
    w_uvt = jnp.transpose(w_ukv3[..., nope:], (1, 2, 0)).reshape(n_heads * d_v, kv_rank).astype(BF16)
    k, vt = _kvup_proj(dm, ckv, w_uk, w_uvt, kr, n_heads)
    return _mla_flash(dm, q, k, vt, n_heads)


def kernel(x, c, ctx, c_ctx, ada_w, ada_b, norm_g, final_g, a_wqkv, a_wo, a_sink, b_wdkv, b_gq, b_gkv, b_wuq,
           b_wukv, b_wo, r_wg, r_bg, r_we, r_be, e_wgu, e_wdn):
    b, s, d = x.shape
    c_len = ctx.shape[1]
    depth = ada_w.shape[0]
    dm = _Dims(b, s, c_len, d)
    assert b + 1 <= MOD_ROWS and dm.t_lat % c_len == 0 and c_len % WINDOW == 0 and s % GRID_W == 0
    n_groups, n_experts = r_wg.shape[2], r_we.shape[2]
    assert n_groups + n_experts <= LANES

    xa = jnp.concatenate([x.reshape(dm.t_lat, d), ctx.reshape(dm.t_ctx, d)], axis=0)
    cc = jnp.concatenate([c, c_ctx[None, :], jnp.zeros((MOD_ROWS - b - 1, d), F32)], axis=0)
    mod = _ada(cc, ada_w, ada_b)
    mod3 = [mod[i].reshape(MOD_ROWS, 1, 6 * d) for i in range(depth)]
    cos_t, sin_t = _rope_tables(s, A_HEAD_DIM, dm.tm)

    e_wgu_b, e_wdn_b = e_wgu.astype(BF16), e_wdn.astype(BF16)

    h = _modulate(dm, xa, norm_g[0, 0], mod3[0], 0, 1)
    out = None
    for i in range(depth):
        last = i == depth - 1
        j = i // 2
        rows = dm.t_lat if last else dm.t_all
        if i % 2 == 0:
            o = _mixer_a(dm, h, a_wqkv[j], a_sink[j], cos_t, sin_t)
            w_o = a_wo[j]
        else:
            o = _mixer_b(dm, h, b_wdkv[j], b_gq[j], b_gkv[j], b_wuq[j], b_wukv[j], b_wo.shape[1], cos_t, sin_t)
            w_o = b_wo[j]
        x1 = _mm_res(dm, o, w_o.astype(BF16), xa, mod3[i], 2, rows)
        w_r = jnp.pad(jnp.concatenate([r_wg[i], r_we[i]], axis=1), ((0, 0), (0, LANES - n_groups - n_experts)))
        b_r = jnp.pad(jnp.concatenate([r_bg[i], r_be[i]]), (0, LANES - n_groups - n_experts)).reshape(1, LANES)
        h2, route = _mod_router(dm, x1, norm_g[i, 1], mod3[i], 3, 4, w_r, b_r, rows, n_groups, n_experts)
        blk_e, src_tok, n_used, dest, n_rows = _dispatch(route, n_experts)
        y_buf = _moe_experts(h2, blk_e, src_tok, n_used, e_wgu_b, e_wdn_b, i, n_rows)
        if last:
            out = _combine(dm, y_buf, dest, x1, route, mod3[i], 5, rows, final_g=final_g)
        else:
            xa, h = _combine(dm, y_buf, dest, x1, route, mod3[i], 5, rows,
                             next_g=norm_g[i + 1, 0], next_mod3=mod3[i + 1])
    return out.reshape(b, s, d)
```

```python
import functools
import math

import jax
import jax.numpy as jnp
from jax import lax
from jax.experimental import pallas as pl
from jax.experimental.pallas import tpu as pltpu

F32 = jnp.float32
BF16 = jnp.bfloat16

EPS = 1e-6
NEG_INF = -1e30
ROPE_BASE = 10000.0
GRID_W = 64
A_HEAD_DIM = 64
WINDOW = 128
MOE_BLOCK = 128
LANES = 128
MOD_ROWS = 8
LOG2E = math.log2(math.e)
VMEM_LIMIT_BYTES = 56 * 1024 * 1024


def _cparams(*sem):
    return pltpu.CompilerParams(dimension_semantics=sem, vmem_limit_bytes=VMEM_LIMIT_BYTES)


def _dot(a, b):
    return jnp.dot(a, b, preferred_element_type=F32)


def _dot_nt(a, b):
    return lax.dot_general(a, b, (((1,), (1,)), ((), ())), preferred_element_type=F32)


def _split_bf16(v):
    hi = v.astype(BF16)
    lo = (v - hi.astype(F32)).astype(BF16)
    return hi, lo


def _rms(x):
    return x * lax.rsqrt(jnp.mean(x * x, axis=-1, keepdims=True) + EPS)


def _modulate_val(x, g, shift, scale):
    return _rms(x) * g * (1.0 + scale) + shift


def _rope128(x, cos, sin_signed):
    lane = lax.broadcasted_iota(jnp.int32, x.shape, 1)
    first_half = (lane & 32) == 0
    partner = jnp.where(first_half, pltpu.roll(x, 96, 1), pltpu.roll(x, 32, 1))
    return x * cos + partner * sin_signed


def _ada_kernel(c_ref, w_ref, b_ref, o_ref):
    c = c_ref[...]
    s = c * jax.nn.sigmoid(c)
    s_hi, s_lo = _split_bf16(s)
    w_hi, w_lo = _split_bf16(w_ref[...])
    r = _dot(jnp.concatenate([s_hi, s_lo], axis=0), w_hi)
    o_ref[...] = r[:MOD_ROWS] + r[MOD_ROWS:] + _dot(s_hi, w_lo) + b_ref[...]


def _ada(cc, ada_w, ada_b):
    depth, d, n = ada_w.shape
    tn = 512
    return pl.pallas_call(
        _ada_kernel,
        grid=(depth, n // tn),
        in_specs=[
            pl.BlockSpec((MOD_ROWS, d), lambda l, j: (0, 0)),
            pl.BlockSpec((None, d, tn), lambda l, j: (l, 0, j)),
            pl.BlockSpec((None, 1, tn), lambda l, j: (l, 0, j)),
        ],
        out_specs=pl.BlockSpec((None, MOD_ROWS, tn), lambda l, j: (l, 0, j)),
        out_shape=jax.ShapeDtypeStruct((depth, MOD_ROWS, n), F32),
        compiler_params=_cparams("arbitrary", "arbitrary"),
        name="ada",
    )(cc, ada_w, ada_b.reshape(depth, 1, n))


class _Dims:
    def __init__(self, b, s, c, d):
        self.b, self.s, self.c, self.d = b, s, c, d
        self.t_lat, self.t_ctx = b * s, b * c
        self.t_all = self.t_lat + self.t_ctx
        self.tm = next(t for t in (512, 256, 128) if s % t == 0 and self.t_ctx % t == 0)

    def modrow(self, i, tm):
        return jnp.minimum(i // (self.s // tm), self.b)

    def ropeblk(self, i, tm):
        return jnp.where(i < self.t_lat // tm, i % (self.s // tm), self.s // tm)

    def mod_spec(self, col, tm, width=None):
        width = self.d if width is None else width
        per = self.d // width
        if width == self.d:
            return pl.BlockSpec((1, 1, width), lambda i, *_: (self.modrow(i, tm), 0, col))
        return pl.BlockSpec((1, 1, width), lambda i, j, *_: (self.modrow(i, tm), 0, col * per + j))


def _rope_tables(s, d_rot, pad_rows):
    rows = s // GRID_W
    row = jnp.repeat(jnp.arange(rows, dtype=F32), GRID_W)
    col = jnp.tile(jnp.arange(GRID_W, dtype=F32), rows)
    n_freq = d_rot // 4
    inv = ROPE_BASE ** (-jnp.arange(n_freq, dtype=F32) / n_freq)
    ang = jnp.concatenate([row[:, None] * inv, col[:, None] * inv], axis=-1)
    cos, sin = jnp.cos(ang), jnp.sin(ang)
    reps = LANES // d_rot
    cos_t = jnp.tile(cos, (1, 2 * reps))
    sin_t = jnp.tile(jnp.concatenate([-sin, sin], axis=-1), (1, reps))
    cos_t = jnp.concatenate([cos_t, jnp.ones((pad_rows, LANES), F32)], axis=0)
    sin_t = jnp.concatenate([sin_t, jnp.zeros((pad_rows, LANES), F32)], axis=0)
    return cos_t, sin_t


def _modulate_kernel(x_ref, g_ref, sh_ref, sc_ref, o_ref):
    o_ref[...] = _modulate_val(x_ref[...], g_ref[...], sh_ref[0], sc_ref[0]).astype(o_ref.dtype)


def _modulate(dm, x, g, mod3, shift_col, scale_col):
    tm = min(dm.tm, 256)
    return pl.pallas_call(
        _modulate_kernel,
        grid=(dm.t_all // tm,),
        in_specs=[
            pl.BlockSpec((tm, dm.d), lambda i: (i, 0)),
            pl.BlockSpec((1, dm.d), lambda i: (0, 0)),
            dm.mod_spec(shift_col, tm),
            dm.mod_spec(scale_col, tm),
        ],
        out_specs=pl.BlockSpec((tm, dm.d), lambda i: (i, 0)),
        out_shape=jax.ShapeDtypeStruct((dm.t_all, dm.d), BF16),
        compiler_params=_cparams("arbitrary"),
        name="modulate",
    )(x, g.reshape(1, dm.d), mod3, mod3)


def _qkv_kernel(a_ref, w_ref, cos_ref, sin_ref, o_ref, *, n_q_blocks, n_rope_blocks, qscale):
    j = pl.program_id(1)
    acc = _dot(a_ref[...], w_ref[...])

    @pl.when(j >= n_rope_blocks)
    def _():
        o_ref[...] = acc.astype(o_ref.dtype)

    @pl.when(j < n_rope_blocks)
    def _():
        scale = jnp.where(j < n_q_blocks, qscale, 1.0).astype(F32)
        cos, sin = cos_ref[...], sin_ref[...]
        for g in range(acc.shape[1] // LANES):
            sl = slice(g * LANES, (g + 1) * LANES)
            o_ref[:, sl] = (_rope128(acc[:, sl], cos, sin) * scale).astype(o_ref.dtype)


def _qkv_proj(dm, h, w, cos_t, sin_t, qd, kvd):
    tm = dm.tm
    n = w.shape[1]
    tn = math.gcd(512, kvd)
    kern = functools.partial(_qkv_kernel, n_q_blocks=qd // tn, n_rope_blocks=(qd + kvd) // tn,
                             qscale=A_HEAD_DIM ** -0.5 * LOG2E)
    return pl.pallas_call(
        kern,
        grid=(dm.t_all // tm, n // tn),
        in_specs=[
            pl.BlockSpec((tm, dm.d), lambda i, j: (i, 0)),
            pl.BlockSpec((dm.d, tn), lambda i, j: (0, j)),
            pl.BlockSpec((tm, LANES), lambda i, j: (dm.ropeblk(i, tm), 0)),
            pl.BlockSpec((tm, LANES), lambda i, j: (dm.ropeblk(i, tm), 0)),
        ],
        out_specs=pl.BlockSpec((tm, tn), lambda i, j: (i, j)),
        out_shape=jax.ShapeDtypeStruct((dm.t_all, n), BF16),
        compiler_params=_cparams("arbitrary", "arbitrary"),
        name="qkv_proj",
    )(h, w, cos_t, sin_t)


def _attn_a_kernel(sink_ref, q_ref, kp_ref, kc_ref, kn_ref, vp_ref, vc_ref, vn_ref, kx_ref, vx_ref, o_ref,
                   *, nb, group):
    i = pl.program_id(1)
    hp = pl.program_id(2)
    npair = group // 2
    w = WINDOW
    hd = A_HEAD_DIM
    n_loc = 3 * w
    nq = npair * w
    lane = lax.broadcasted_iota(jnp.int32, (1, LANES), 1)
    lo_mask = lane < hd
    lane_q = lax.broadcasted_iota(jnp.int32, (1, nq), 1)

    k_all = jnp.concatenate([kp_ref[...], kc_ref[...], kn_ref[...], kx_ref[...]], axis=0).astype(F32)
    v_all = jnp.concatenate([vp_ref[...], vc_ref[...], vn_ref[...], vx_ref[...]], axis=0).astype(F32)
    v_t = v_all.T
    zeros_half = jnp.zeros((hd, v_t.shape[1]), F32)

    c = lax.broadcasted_iota(jnp.int32, (n_loc, w), 0)
    r = lax.broadcasted_iota(jnp.int32, (n_loc, w), 1)
    rel = c - w - r
    valid = ((rel >= -w) & (rel <= w) & ((c >= w) | (i > 0)) & ((c < 2 * w) | (i < nb - 1)) & (i < nb))
    bias = jnp.where(valid, 0.0, NEG_INF).astype(F32)
    bias = jnp.concatenate([bias] * npair, axis=1)

    scores, values = [], []
    for par in range(2):
        if par == 0:
            k_left = jnp.where(lo_mask, k_all, 0.0)
            k_right = pltpu.roll(k_left, hd, 1)
        else:
            k_right = jnp.where(lo_mask, 0.0, k_all)
            k_left = pltpu.roll(k_right, hd, 1)
        v_h = v_t[par * hd:(par + 1) * hd]
        values.append((jnp.concatenate([v_h, zeros_half], axis=0).astype(BF16),
                       jnp.concatenate([zeros_half, v_h], axis=0).astype(BF16)))
        base = par * group * hd
        qs = jnp.concatenate([q_ref[:, base + pp * LANES: base + (pp + 1) * LANES] for pp in range(npair)], axis=0)
        qs_t = qs.T
        scores.append((_dot(k_left.astype(BF16), qs_t), _dot(k_right.astype(BF16), qs_t)))

    for par in range(2):
        base = par * group * hd
        o_pair_t = None
        for side in range(2):
            s = scores[par][side]
            head0 = (2 * hp + par) * group + side
            sink = jnp.full((1, nq), sink_ref[head0], F32)
            for pp in range(1, npair):
                sink = jnp.where(lane_q >= pp * w, sink_ref[head0 + 2 * pp], sink)
            sink = sink * LOG2E
            sl = s[:n_loc] + bias
            sc = s[n_loc:]
            m = jnp.maximum(jnp.maximum(jnp.max(sl, axis=0, keepdims=True), jnp.max(sc, axis=0, keepdims=True)),
                            sink)
            el = jnp.exp2(sl - m)
            ec = jnp.exp2(sc - m)
            den = jnp.sum(el, axis=0, keepdims=True) + jnp.sum(ec, axis=0, keepdims=True) + jnp.exp2(sink - m)
            e = jnp.concatenate([el, ec], axis=0).astype(BF16)
            o_t = _dot(values[par][side], e) / den
            o_pair_t = o_t if o_pair_t is None else o_pair_t + o_t
        o_pair = o_pair_t.T
        for pp in range(npair):
            o_ref[:, base + pp * LANES: base + (pp + 1) * LANES] = o_pair[pp * w:(pp + 1) * w].astype(o_ref.dtype)


def _attn_a(dm, z, sink, n_heads, n_kv):
    group = n_heads // n_kv
    qd, kvd = n_heads * A_HEAD_DIM, n_kv * A_HEAD_DIM
    w = WINDOW
    nb = dm.s // w
    ncb = dm.c // w
    lat_blocks = dm.t_lat // w
    qw = 2 * group * A_HEAD_DIM
    kcol, vcol = qd // LANES, (qd + kvd) // LANES

    def qrow(b, i):
        return jnp.where(i < nb, b * nb + i, lat_blocks + b * ncb + (i - nb))

    def krow(b, i, off):
        return b * nb + jnp.clip(jnp.minimum(i, nb - 1) + off, 0, nb - 1)

    def kv_spec(col, off):
        return pl.BlockSpec((w, LANES), lambda b, i, hp, s_: (krow(b, i, off), col + hp))

    def ctx_spec(col):
        return pl.BlockSpec((dm.c, LANES), lambda b, i, hp, s_: (dm.t_lat // dm.c + b, col + hp))

    grid_spec = pltpu.PrefetchScalarGridSpec(
        num_scalar_prefetch=1,
        grid=(dm.b, nb + ncb, n_kv // 2),
        in_specs=[
            pl.BlockSpec((w, qw), lambda b, i, hp, s_: (qrow(b, i), hp)),
            kv_spec(kcol, -1), kv_spec(kcol, 0), kv_spec(kcol, 1),
            kv_spec(vcol, -1), kv_spec(vcol, 0), kv_spec(vcol, 1),
            ctx_spec(kcol), ctx_spec(vcol),
        ],
        out_specs=pl.BlockSpec((w, qw), lambda b, i, hp, s_: (qrow(b, i), hp)),
    )
    return pl.pallas_call(
        functools.partial(_attn_a_kernel, nb=nb, group=group),
        grid_spec=grid_spec,
        out_shape=jax.ShapeDtypeStruct((dm.t_all, qd), BF16),
        compiler_params=_cparams("arbitrary", "arbitrary", "arbitrary"),
        name="attn_window",
    )(sink, z, z, z, z, z, z, z, z, z)


def _mm_res_kernel(a_ref, w_ref, res_ref, gate_ref, o_ref):
    o_ref[...] = res_ref[...] + gate_ref[0] * _dot(a_ref[...], w_ref[...])


def _mm_res(dm, a, w, res, mod3, gate_col, rows):
    tm = 2 * dm.tm if (rows % (2 * dm.tm) == 0 and dm.s % (2 * dm.tm) == 0) else dm.tm
    k, n = w.shape
    tn = min(512, n)
    return pl.pallas_call(
        _mm_res_kernel,
        grid=(rows // tm, n // tn),
        in_specs=[
            pl.BlockSpec((tm, k), lambda i, j: (i, 0)),
            pl.BlockSpec((k, tn), lambda i, j: (0, j)),
            pl.BlockSpec((tm, tn), lambda i, j: (i, j)),
            dm.mod_spec(gate_col, tm, tn),
        ],
        out_specs=pl.BlockSpec((tm, tn), lambda i, j: (i, j)),
        out_shape=jax.ShapeDtypeStruct((rows, n), F32),
        compiler_params=_cparams("arbitrary", "arbitrary"),
        name="out_proj_residual",
    )(a, w, res, mod3)


def _mod_router_kernel(x_ref, g_ref, sh_ref, sc_ref, wh_ref, wl_ref, br_ref, h_ref, route_ref, *, n_groups, n_experts):
    h = _modulate_val(x_ref[...], g_ref[...], sh_ref[0], sc_ref[0])
    h_ref[...] = h
    h_hi, h_lo = _split_bf16(h)
    wh = wh_ref[...]
    lg = _dot(h_hi, wh) + _dot(h_lo, wh) + _dot(h_hi, wl_ref[...]) + br_ref[...]

    per = n_experts // n_groups
    lane = lax.broadcasted_iota(jnp.int32, lg.shape, 1)
    lane_f = lane.astype(F32)
    big = float(LANES)

    def first_argmax(v, vmax):
        return jnp.min(jnp.where(v == vmax, lane_f, big), axis=1, keepdims=True)

    is_g = lane < n_groups
    g_l = jnp.where(is_g, lg, NEG_INF)
    gmax = jnp.max(g_l, axis=1, keepdims=True)
    grp = first_argmax(g_l, gmax)
    p_grp = 1.0 / jnp.sum(jnp.where(is_g, jnp.exp(g_l - gmax), 0.0), axis=1, keepdims=True)

    lo = n_groups + grp * per
    e_l = jnp.where((lane_f >= lo) & (lane_f < lo + per), lg, NEG_INF)
    t1 = jnp.max(e_l, axis=1, keepdims=True)
    i1 = first_argmax(e_l, t1)
    e_l2 = jnp.where(lane_f == i1, NEG_INF, e_l)
    t2 = jnp.max(e_l2, axis=1, keepdims=True)
    i2 = first_argmax(e_l2, t2)
    ratio = jnp.exp(t2 - t1)
    gate1 = p_grp / (1.0 + ratio)
    gate2 = gate1 * ratio
    route = jnp.where(lane == 0, i1 - n_groups,
                      jnp.where(lane == 1, i2 - n_groups,
                                jnp.where(lane == 2, gate1, jnp.where(lane == 3, gate2, 0.0))))
    route_ref[...] = route


def _mod_router(dm, x, g, mod3, shift_col, scale_col, w_r, b_r, rows, n_groups, n_experts):
    tm = min(dm.tm, 256)
    w_hi, w_lo = _split_bf16(w_r)
    kern = functools.partial(_mod_router_kernel, n_groups=n_groups, n_experts=n_experts)
    return pl.pallas_call(
        kern,
        grid=(rows // tm,),
        in_specs=[
            pl.BlockSpec((tm, dm.d), lambda i: (i, 0)),
            pl.BlockSpec((1, dm.d), lambda i: (0, 0)),
            dm.mod_spec(shift_col, tm),
            dm.mod_spec(scale_col, tm),
            pl.BlockSpec((dm.d, LANES), lambda i: (0, 0)),
            pl.BlockSpec((dm.d, LANES), lambda i: (0, 0)),
            pl.BlockSpec((1, LANES), lambda i: (0, 0)),
        ],
        out_specs=[
            pl.BlockSpec((tm, dm.d), lambda i: (i, 0)),
            pl.BlockSpec((tm, LANES), lambda i: (i, 0)),
        ],
        out_shape=[
            jax.ShapeDtypeStruct((rows, dm.d), F32),
            jax.ShapeDtypeStruct((rows, LANES), F32),
        ],
        compiler_params=_cparams("arbitrary"),
        name="modulate_router",
    )(x, g.reshape(1, dm.d), mod3, mod3, w_hi, w_lo, b_r)


def _dispatch(route, n_experts):
    eid = route[:, :2].astype(jnp.int32).reshape(-1)
    n_assign = eid.shape[0]
    order = jnp.argsort(eid).astype(jnp.int32)
    rank = jnp.argsort(order).astype(jnp.int32)
    onehot = eid[:, None] == jnp.arange(n_experts, dtype=jnp.int32)[None, :]
    counts = jnp.sum(onehot, axis=0, dtype=jnp.int32)
    padded = (counts + MOE_BLOCK - 1) // MOE_BLOCK * MOE_BLOCK
    p_end = jnp.cumsum(padded)
    p_start = p_end - padded
    c_start = jnp.cumsum(counts) - counts
    dest = rank + jnp.sum(jnp.where(onehot, (p_start - c_start)[None, :], 0), axis=1, dtype=jnp.int32)
    n_rows = (n_assign + n_experts * (MOE_BLOCK - 1) + MOE_BLOCK - 1) // MOE_BLOCK * MOE_BLOCK
    n_blocks = n_rows // MOE_BLOCK
    blk_start = jnp.arange(n_blocks, dtype=jnp.int32) * MOE_BLOCK
    blk_e = jnp.minimum(jnp.sum(p_end[None, :] <= blk_start[:, None], axis=1, dtype=jnp.int32), n_experts - 1)
    off = (blk_start - p_start[blk_e])[:, None] + jnp.arange(MOE_BLOCK, dtype=jnp.int32)[None, :]
    pos = jnp.clip(c_start[blk_e][:, None] + off, 0, n_assign - 1)
    src_tok = jnp.where(off < counts[blk_e][:, None], order[pos] // 2, 0).reshape(n_rows).astype(jnp.int32)
    n_used = (p_end[-1:] // MOE_BLOCK).astype(jnp.int32)
    return blk_e, src_tok, n_used, dest, n_rows


def _moe_kernel(blk_e_ref, src_ref, nused_ref, h_hbm, wgu_ref, wdn_ref, y_ref, xb, sem, *, d_exp):
    del blk_e_ref
    b = pl.program_id(0)
    n_used = nused_ref[0]
    slot = b % 2

    def row_copies(blk, slot_):
        return [pltpu.make_async_copy(h_hbm.at[pl.ds(src_ref[blk * MOE_BLOCK + r], 1), :],
                                      xb.at[slot_, pl.ds(r, 1), :], sem.at[slot_])
                for r in range(MOE_BLOCK)]

    @pl.when(b == 0)
    def _():
        for cp in row_copies(0, 0):
            cp.start()

    @pl.when(b >= n_used)
    def _():
        y_ref[...] = jnp.zeros(y_ref.shape, y_ref.dtype)

    @pl.when(b < n_used)
    def _():
        for cp in row_copies(b, slot):
            cp.wait()

        @pl.when(b + 1 < n_used)
        def _():
            for cp in row_copies(b + 1, 1 - slot):
                cp.start()

        xrow = xb[slot].astype(BF16)
        gu = _dot(xrow, wgu_ref[...])
        g, u = gu[:, :d_exp], gu[:, d_exp:]
        act = (g * jax.nn.sigmoid(g) * u).astype(BF16)
        y_ref[...] = _dot(act, wdn_ref[...])


def _moe_experts(h, blk_e, src_tok, n_used, w_gu, w_dn, layer, n_rows):
    _, _, d, f2 = w_gu.shape
    d_exp = f2 // 2
    n_blocks = n_rows // MOE_BLOCK
    grid_spec = pltpu.PrefetchScalarGridSpec(
        num_scalar_prefetch=3,
        grid=(n_blocks,),
        in_specs=[
            pl.BlockSpec(memory_space=pl.ANY),
            pl.BlockSpec((None, None, d, f2), lambda b, be, st, nu: (layer, be[b], 0, 0)),
            pl.BlockSpec((None, None, d_exp, d), lambda b, be, st, nu: (layer, be[b], 0, 0)),
        ],
        out_specs=pl.BlockSpec((MOE_BLOCK, d), lambda b, be, st, nu: (b, 0)),
        scratch_shapes=[
            pltpu.VMEM((2, MOE_BLOCK, d), F32),
            pltpu.SemaphoreType.DMA((2,)),
        ],
    )
    return pl.pallas_call(
        functools.partial(_moe_kernel, d_exp=d_exp),
        grid_spec=grid_spec,
        out_shape=jax.ShapeDtypeStruct((n_rows, d), F32),
        compiler_params=_cparams("arbitrary"),
        name="moe_experts",
    )(blk_e, src_tok, n_used, h, w_gu, w_dn)


def _combine_kernel(dest_ref, y_hbm, x_ref, route_ref, g2_ref, *refs, tc, n_blocks, final):
    if final:
        fg_ref, out_ref, yb, sem = refs
    else:
        ng_ref, sh_ref, sc_ref, xo_ref, h_ref, yb, sem = refs
    t = pl.program_id(0)
    slot = t % 2

    def row_copies(blk, slot_):
        return [pltpu.make_async_copy(y_hbm.at[pl.ds(dest_ref[(blk * tc + r) * 2 + k], 1), :],
                                      yb.at[slot_, k, pl.ds(r, 1), :], sem.at[slot_])
                for r in range(tc) for k in range(2)]

    @pl.when(t == 0)
    def _():
        for cp in row_copies(0, 0):
            cp.start()

    for cp in row_copies(t, slot):
        cp.wait()

    @pl.when(t + 1 < n_blocks)
    def _():
        for cp in row_copies(t + 1, 1 - slot):
            cp.start()

    route = route_ref[...]
    y = route[:, 2:3] * yb[slot, 0] + route[:, 3:4] * yb[slot, 1]
    xn = x_ref[...] + g2_ref[0] * y
    if final:
        out_ref[...] = _rms(xn) * fg_ref[...]
    else:
        xo_ref[...] = xn
        h_ref[...] = _modulate_val(xn, ng_ref[...], sh_ref[0], sc_ref[0]).astype(h_ref.dtype)


def _combine(dm, y_buf, dest, x, route, mod3, gate_col, rows, *, final_g=None, next_g=None, next_mod3=None):
    tc = MOE_BLOCK
    n_blocks = rows // tc
    final = final_g is not None
    row_spec = pl.BlockSpec((tc, dm.d), lambda t, d_: (t, 0))
    vec_spec = pl.BlockSpec((1, dm.d), lambda t, d_: (0, 0))
    in_specs = [
        pl.BlockSpec(memory_space=pl.ANY),
        row_spec,
        pl.BlockSpec((tc, LANES), lambda t, d_: (t, 0)),
        dm.mod_spec(gate_col, tc),
    ]
    if final:
        in_specs += [vec_spec]
        args = (final_g.reshape(1, dm.d),)
        out_specs = row_spec
        out_shape = jax.ShapeDtypeStruct((rows, dm.d), F32)
    else:
        in_specs += [vec_spec, dm.mod_spec(0, tc), dm.mod_spec(1, tc)]
        args = (next_g.reshape(1, dm.d), next_mod3, next_mod3)
        out_specs = [row_spec, row_spec]
        out_shape = [jax.ShapeDtypeStruct((rows, dm.d), F32), jax.ShapeDtypeStruct((rows, dm.d), BF16)]
    grid_spec = pltpu.PrefetchScalarGridSpec(
        num_scalar_prefetch=1,
        grid=(n_blocks,),
        in_specs=in_specs,
        out_specs=out_specs,
        scratch_shapes=[
            pltpu.VMEM((2, 2, tc, dm.d), F32),
            pltpu.SemaphoreType.DMA((2,)),
        ],
    )
    return pl.pallas_call(
        functools.partial(_combine_kernel, tc=tc, n_blocks=n_blocks, final=final),
        grid_spec=grid_spec,
        out_shape=out_shape,
        compiler_params=_cparams("arbitrary"),
        name="moe_combine",
    )(dest, y_buf, x, route, mod3, *args)


def _dkv_kernel(a_ref, wq_ref, wkv_ref, gq_ref, gkv_ref, cos_ref, sin_ref, zq_ref, ckv_ref, kr_ref, *, kv_rank):
    a = a_ref[...]
    zq_ref[...] = (_rms(_dot(a, wq_ref[...])) * gq_ref[...]).astype(zq_ref.dtype)
    zkv = _dot(a, wkv_ref[...])
    ckv_ref[...] = (_rms(zkv[:, :kv_rank]) * gkv_ref[...]).astype(ckv_ref.dtype)
    kr_ref[...] = _rope128(zkv[:, kv_rank:], cos_ref[...], sin_ref[...]).astype(kr_ref.dtype)


def _dkv_proj(dm, h, wq, wkv, gq, gkv, cos_t, sin_t):
    tm = dm.tm
    q_rank = wq.shape[1]
    kv_rank = wkv.shape[1] - LANES
    rope_spec = pl.BlockSpec((tm, LANES), lambda i: (dm.ropeblk(i, tm), 0))
    return pl.pallas_call(
        functools.partial(_dkv_kernel, kv_rank=kv_rank),
        grid=(dm.t_all // tm,),
        in_specs=[
            pl.BlockSpec((tm, dm.d), lambda i: (i, 0)),
            pl.BlockSpec((dm.d, q_rank), lambda i: (0, 0)),
            pl.BlockSpec((dm.d, kv_rank + LANES), lambda i: (0, 0)),
            pl.BlockSpec((1, q_rank), lambda i: (0, 0)),
            pl.BlockSpec((1, kv_rank), lambda i: (0, 0)),
            rope_spec, rope_spec,
        ],
        out_specs=[
            pl.BlockSpec((tm, q_rank), lambda i: (i, 0)),
            pl.BlockSpec((tm, kv_rank), lambda i: (i, 0)),
            pl.BlockSpec((tm, LANES), lambda i: (i, 0)),
        ],
        out_shape=[
            jax.ShapeDtypeStruct((dm.t_all, q_rank), BF16),
            jax.ShapeDtypeStruct((dm.t_all, kv_rank), BF16),
            jax.ShapeDtypeStruct((dm.t_all, LANES), BF16),
        ],
        compiler_params=_cparams("arbitrary"),
        name="mla_down_proj",
    )(h, wq, wkv, gq.reshape(1, -1), gkv.reshape(1, -1), cos_t, sin_t)


def _qup_kernel(a_ref, w_ref, cos_ref, sin_ref, o_ref, *, qscale):
    acc = _dot(a_ref[...], w_ref[...])
    cos, sin = cos_ref[...], sin_ref[...]
    for g in range(acc.shape[1] // LANES):
        sl = slice(g * LANES, (g + 1) * LANES)
        x = acc[:, sl]
        if g % 2 == 1:
            x = _rope128(x, cos, sin)
        o_ref[:, sl] = (x * qscale).astype(o_ref.dtype)


def _qup_proj(dm, zq, w, cos_t, sin_t, qscale):
    tm = dm.tm
    k, n = w.shape
    tn = min(1024, n)
    rope_spec = pl.BlockSpec((tm, LANES), lambda i, j: (dm.ropeblk(i, tm), 0))
    return pl.pallas_call(
        functools.partial(_qup_kernel, qscale=qscale),
        grid=(dm.t_lat // tm, n // tn),
        in_specs=[
            pl.BlockSpec((tm, k), lambda i, j: (i, 0)),
            pl.BlockSpec((k, tn), lambda i, j: (0, j)),
            rope_spec, rope_spec,
        ],
        out_specs=pl.BlockSpec((tm, tn), lambda i, j: (i, j)),
        out_shape=jax.ShapeDtypeStruct((dm.t_lat, n), BF16),
        compiler_params=_cparams("arbitrary", "arbitrary"),
        name="mla_q_up",
    )(zq, w, cos_t, sin_t)


def _kvup_kernel(a_ref, wk_ref, wvt_ref, kr_ref, k_ref, vt_ref, *, heads):
    a = a_ref[...]
    kn = _dot(a, wk_ref[...])
    kr = kr_ref[...]
    for h in range(heads):
        k_ref[:, h * 2 * LANES: h * 2 * LANES + LANES] = kn[:, h * LANES: (h + 1) * LANES].astype(k_ref.dtype)
        k_ref[:, h * 2 * LANES + LANES: (h + 1) * 2 * LANES] = kr
    vt_ref[...] = _dot_nt(wvt_ref[...], a).astype(vt_ref.dtype)


def _kvup_proj(dm, ckv, w_uk, w_uvt, kr, n_heads):
    tm = dm.tm
    kv_rank = ckv.shape[1]
    hb = math.gcd(4, n_heads)
    return pl.pallas_call(
        functools.partial(_kvup_kernel, heads=hb),
        grid=(dm.t_all // tm, n_heads // hb),
        in_specs=[
            pl.BlockSpec((tm, kv_rank), lambda i, j: (i, 0)),
            pl.BlockSpec((kv_rank, hb * LANES), lambda i, j: (0, j)),
            pl.BlockSpec((hb * LANES, kv_rank), lambda i, j: (j, 0)),
            pl.BlockSpec((tm, LANES), lambda i, j: (i, 0)),
        ],
        out_specs=[
            pl.BlockSpec((tm, hb * 2 * LANES), lambda i, j: (i, j)),
            pl.BlockSpec((hb * LANES, tm), lambda i, j: (j, i)),
        ],
        out_shape=[
            jax.ShapeDtypeStruct((dm.t_all, n_heads * 2 * LANES), BF16),
            jax.ShapeDtypeStruct((n_heads * LANES, dm.t_all), BF16),
        ],
        compiler_params=_cparams("arbitrary", "arbitrary"),
        name="mla_kv_up",
    )(ckv, w_uk, w_uvt, kr)


def _mla_flash_kernel(q_ref, kl_ref, kc_ref, vl_ref, vc_ref, o_ref, m_sc, l_sc, acc_sc, sa_sc, sb_sc, pa_sc, pb_sc,
                      qt_sc, *, tk, n_chunks):
    tq = q_ref.shape[0]
    half = tq // 2
    m_sc[...] = jnp.full(m_sc.shape, NEG_INF, F32)
    l_sc[...] = jnp.zeros(l_sc.shape, F32)
    acc_sc[...] = jnp.zeros(acc_sc.shape, F32)
    pb_sc[...] = jnp.zeros(pb_sc.shape, pb_sc.dtype)
    qt_sc[...] = q_ref[...].T

    def k_chunk(ci):
        return kl_ref[pl.ds(pl.multiple_of(ci * tk, tk), tk), :]

    def vt_chunk(ci):
        return vl_ref[:, pl.ds(pl.multiple_of(ci * tk, tk), tk)]

    def softmax(s, sl):
        m_prev = m_sc[:, sl]
        m_new = jnp.maximum(m_prev, jnp.max(s, axis=0, keepdims=True))
        alpha = jnp.exp2(m_prev - m_new)
        p = jnp.exp2(s - m_new)
        l_sc[:, sl] = alpha * l_sc[:, sl] + jnp.sum(p, axis=0, keepdims=True)
        m_sc[:, sl] = m_new
        return alpha, p.astype(BF16)

    def stage(c_prev, c_next, s_cur, s_next, p_prev, p_cur):
        vt = vt_chunk(c_prev)
        kn = k_chunk(c_next)
        for hf in range(2):
            sl = slice(hf * half, (hf + 1) * half)
            pv = _dot(vt, p_prev[:, sl])
            s_next[:, sl] = _dot(kn, qt_sc[:, sl])
            alpha, p = softmax(s_cur[:, sl], sl)
            p_cur[:, sl] = p
            acc_sc[:, sl] = alpha * (acc_sc[:, sl] + pv)

    sa_sc[...] = _dot(k_chunk(0), qt_sc[...])

    def body(j, carry):
        c0 = 2 * j
        stage(jnp.maximum(c0 - 1, 0), c0 + 1, sa_sc, sb_sc, pb_sc, pa_sc)
        stage(c0, jnp.minimum(c0 + 2, n_chunks - 1), sb_sc, sa_sc, pa_sc, pb_sc)
        return carry

    lax.fori_loop(0, n_chunks // 2, body, 0)
    pv = _dot(vt_chunk(n_chunks - 1), pb_sc[...])
    alpha, p = softmax(_dot(kc_ref[...], qt_sc[...]), slice(None))
    acc = alpha * (acc_sc[...] + pv) + _dot(vc_ref[...], p)
    o_ref[...] = (acc / l_sc[...]).T.astype(o_ref.dtype)


def _mla_flash(dm, q, k, vt, n_heads):
    tq = min(512, dm.s)
    tk = min(512, dm.s // 2)
    nq = dm.s // tq
    n_chunks = dm.s // tk
    assert n_chunks % 2 == 0 and tq % (2 * LANES) == 0
    return pl.pallas_call(
        functools.partial(_mla_flash_kernel, tk=tk, n_chunks=n_chunks),
        grid=(dm.b, n_heads, nq),
        in_specs=[
            pl.BlockSpec((tq, 2 * LANES), lambda b, h, i: (b * nq + i, h)),
            pl.BlockSpec((dm.s, 2 * LANES), lambda b, h, i: (b, h)),
            pl.BlockSpec((dm.c, 2 * LANES), lambda b, h, i: (dm.t_lat // dm.c + b, h)),
            pl.BlockSpec((LANES, dm.s), lambda b, h, i: (h, b)),
            pl.BlockSpec((LANES, dm.c), lambda b, h, i: (h, dm.t_lat // dm.c + b)),
        ],
        out_specs=pl.BlockSpec((tq, LANES), lambda b, h, i: (b * nq + i, h)),
        out_shape=jax.ShapeDtypeStruct((dm.t_lat, n_heads * LANES), BF16),
        scratch_shapes=[
            pltpu.VMEM((1, tq), F32),
            pltpu.VMEM((1, tq), F32),
            pltpu.VMEM((LANES, tq), F32),
            pltpu.VMEM((tk, tq), F32),
            pltpu.VMEM((tk, tq), F32),
            pltpu.VMEM((tk, tq), BF16),
            pltpu.VMEM((tk, tq), BF16),
            pltpu.VMEM((2 * LANES, tq), BF16),
        ],
        compiler_params=_cparams("arbitrary", "arbitrary", "arbitrary"),
        name="mla_flash",
    )(q, k, k, vt, vt)


def _mixer_a(dm, h, w_qkv, sink, cos_t, sin_t):
    n_heads = sink.shape[0]
    n_kv = (w_qkv.shape[1] // A_HEAD_DIM - n_heads) // 2
    assert (n_heads // n_kv) % 2 == 0 and n_kv % 2 == 0
    z = _qkv_proj(dm, h, w_qkv.astype(BF16), cos_t, sin_t, n_heads * A_HEAD_DIM, n_kv * A_HEAD_DIM)
    return _attn_a(dm, z, sink, n_heads, n_kv)


def _mixer_b(dm, h, w_dkv, g_q, g_kv, w_uq, w_ukv, wo_rows, cos_t, sin_t):
    q_rank, kv_rank = g_q.shape[0], g_kv.shape[0]
    rope = w_dkv.shape[1] - q_rank - kv_rank
    n_heads = (w_uq.shape[1] - w_ukv.shape[1] + wo_rows) // rope
    nope = (w_ukv.shape[1] - wo_rows) // n_heads
    d_v = wo_rows // n_heads
    assert rope == A_HEAD_DIM and nope == LANES and d_v == LANES
    wq = w_dkv[:, :q_rank].astype(BF16)
    wkv = jnp.pad(w_dkv[:, q_rank:], ((0, 0), (0, LANES - rope))).astype(BF16)
    zq, ckv, kr = _dkv_proj(dm, h, wq, wkv, g_q, g_kv, cos_t, sin_t)
    w_uq_p = jnp.pad(w_uq.reshape(q_rank, n_heads, nope + rope), ((0, 0), (0, 0), (0, LANES - rope)))
    w_uq_p = w_uq_p.reshape(q_rank, n_heads * 2 * LANES).astype(BF16)
    qscale = (nope + rope) ** -0.5 * math.log2(math.e)
    q = _qup_proj(dm, zq, w_uq_p, cos_t, sin_t, qscale)
    w_ukv3 = w_ukv.reshape(kv_rank, n_heads, nope + d_v)
    w_uk = w_ukv3[..., :nope].reshape(kv_rank, n_heads * nope).astype(BF16)
    w_uvt = jnp.transpose(w_ukv3[..., nope:], (1, 2, 0)).reshape(n_heads * d_v, kv_rank).astype(BF16)
    k, vt = _kvup_proj(dm, ckv, w_uk, w_uvt, kr, n_heads)
    return _mla_flash(dm, q, k, vt, n_heads)


def kernel(x, c, ctx, c_ctx, ada_w, ada_b, norm_g, final_g, a_wqkv, a_wo, a_sink, b_wdkv, b_gq, b_gkv, b_wuq,
           b_wukv, b_wo, r_wg, r_bg, r_we, r_be, e_wgu, e_wdn):
    b, s, d = x.shape
    c_len = ctx.shape[1]
    depth = ada_w.shape[0]
    dm = _Dims(b, s, c_len, d)
    assert b + 1 <= MOD_ROWS and dm.t_lat % c_len == 0 and c_len % WINDOW == 0 and s % GRID_W == 0
    n_groups, n_experts = r_wg.shape[2], r_we.shape[2]
    assert n_groups + n_experts <= LANES

    xa = jnp.concatenate([x.reshape(dm.t_lat, d), ctx.reshape(dm.t_ctx, d)], axis=0)
    cc = jnp.concatenate([c, c_ctx[None, :], jnp.zeros((MOD_ROWS - b - 1, d), F32)], axis=0)
    mod = _ada(cc, ada_w, ada_b)
    mod3 = [mod[i].reshape(MOD_ROWS, 1, 6 * d) for i in range(depth)]
    cos_t, sin_t = _rope_tables(s, A_HEAD_DIM, dm.tm)

    e_wgu_b, e_wdn_b = e_wgu.astype(BF16), e_wdn.astype(BF16)

    h = _modulate(dm, xa, norm_g[0, 0], mod3[0], 0, 1)
    out = None
    for i in range(depth):
        last = i == depth - 1
        j = i // 2
        rows = dm.t_lat if last else dm.t_all
        if i % 2 == 0:
            o = _mixer_a(dm, h, a_wqkv[j], a_sink[j], cos_t, sin_t)
            w_o = a_wo[j]
        else:
            o = _mixer_b(dm, h, b_wdkv[j], b_gq[j], b_gkv[j], b_wuq[j], b_wukv[j], b_wo.shape[1], cos_t, sin_t)
            w_o = b_wo[j]
        x1 = _mm_res(dm, o, w_o.astype(BF16), xa, mod3[i], 2, rows)
        w_r = jnp.pad(jnp.concatenate([r_wg[i], r_we[i]], axis=1), ((0, 0), (0, LANES - n_groups - n_experts)))
        b_r = jnp.pad(jnp.concatenate([r_bg[i], r_be[i]]), (0, LANES - n_groups - n_experts)).reshape(1, LANES)
        h2, route = _mod_router(dm, x1, norm_g[i, 1], mod3[i], 3, 4, w_r, b_r, rows, n_groups, n_experts)
        blk_e, src_tok, n_used, dest, n_rows = _dispatch(route, n_experts)
        y_buf = _moe_experts(h2, blk_e, src_tok, n_used, e_wgu_b, e_wdn_b, i, n_rows)
        if last:
            out = _combine(dm, y_buf, dest, x1, route, mod3[i], 5, rows, final_g=final_g)
        else:
            xa, h = _combine(dm, y_buf, dest, x1, route, mod3[i], 5, rows,
                             next_g=norm_g[i + 1, 0], next_mod3=mod3[i + 1])
    return out.reshape(b, s, d)
```

```python
import functools
import math

import jax
import jax.numpy as jnp
from jax import lax
from jax.experimental import pallas as pl
from jax.experimental.pallas import tpu as pltpu

F32 = jnp.float32
BF16 = jnp.bfloat16

EPS = 1e-6
NEG_INF = -1e30
ROPE_BASE = 10000.0
GRID_W = 64
A_HEAD_DIM = 64
WINDOW = 128
MOE_BLOCK = 128
LANES = 128
MOD_ROWS = 8
LOG2E = math.log2(math.e)
WEIGHT_CAST_CHUNKS = 8
MOD_CHUNK_ROWS = 32
VMEM_LIMIT_BYTES = 56 * 1024 * 1024


def _cparams(*sem):
    return pltpu.CompilerParams(dimension_semantics=sem, vmem_limit_bytes=VMEM_LIMIT_BYTES)


def _dot(a, b):
    return jnp.dot(a, b, preferred_element_type=F32)


def _dot_nt(a, b):
    return lax.dot_general(a, b, (((1,), (1,)), ((), ())), preferred_element_type=F32)


def _split_bf16(v):
    hi = v.astype(BF16)
    lo = (v - hi.astype(F32)).astype(BF16)
    return hi, lo


def _rms(x):
    return x * lax.rsqrt(jnp.mean(x * x, axis=-1, keepdims=True) + EPS)


def _modulate_val(x, g, shift, scale):
    return _rms(x) * g * (1.0 + scale) + shift


def _rope128(x, cos, sin_signed):
    lane = lax.broadcasted_iota(jnp.int32, x.shape, 1)
    first_half = (lane & 32) == 0
    partner = jnp.where(first_half, pltpu.roll(x, 96, 1), pltpu.roll(x, 32, 1))
    return x * cos + partner * sin_signed


def _ada_kernel(c_ref, w_ref, b_ref, o_ref):
    c = c_ref[...]
    s = c * jax.nn.sigmoid(c)
    s_hi, s_lo = _split_bf16(s)
    w_hi, w_lo = _split_bf16(w_ref[...])
    r = _dot(jnp.concatenate([s_hi, s_lo], axis=0), w_hi)
    o_ref[...] = r[:MOD_ROWS] + r[MOD_ROWS:] + _dot(s_hi, w_lo) + b_ref[...]


def _ada(cc, ada_w, ada_b):
    depth, d, n = ada_w.shape
    tn = 512
    return pl.pallas_call(
        _ada_kernel,
        grid=(depth, n // tn),
        in_specs=[
            pl.BlockSpec((MOD_ROWS, d), lambda l, j: (0, 0)),
            pl.BlockSpec((None, d, tn), lambda l, j: (l, 0, j)),
            pl.BlockSpec((None, 1, tn), lambda l, j: (l, 0, j)),
        ],
        out_specs=pl.BlockSpec((None, MOD_ROWS, tn), lambda l, j: (l, 0, j)),
        out_shape=jax.ShapeDtypeStruct((depth, MOD_ROWS, n), F32),
        compiler_params=_cparams("arbitrary", "arbitrary"),
        name="ada",
    )(cc, ada_w, ada_b.reshape(depth, 1, n))


class _Dims:
    def __init__(self, b, s, c, d):
        self.b, self.s, self.c, self.d = b, s, c, d
        self.t_lat, self.t_ctx = b * s, b * c
        self.t_all = self.t_lat + self.t_ctx
        self.tm = next(t for t in (512, 256, 128) if s % t == 0 and self.t_ctx % t == 0)

    def modrow(self, i, tm):
        return jnp.minimum(i // (self.s // tm), self.b)

    def ropeblk(self, i, tm):
        return jnp.where(i < self.t_lat // tm, i % (self.s // tm), self.s // tm)

    def mod_spec(self, col, tm, width=None):
        width = self.d if width is None else width
        per = self.d // width
        if width == self.d:
            return pl.BlockSpec((1, 1, width), lambda i, *_: (self.modrow(i, tm), 0, col))
        return pl.BlockSpec((1, 1, width), lambda i, j, *_: (self.modrow(i, tm), 0, col * per + j))


def _rope_tables(s, d_rot, pad_rows):
    rows = s // GRID_W
    row = jnp.repeat(jnp.arange(rows, dtype=F32), GRID_W)
    col = jnp.tile(jnp.arange(GRID_W, dtype=F32), rows)
    n_freq = d_rot // 4
    inv = ROPE_BASE ** (-jnp.arange(n_freq, dtype=F32) / n_freq)
    ang = jnp.concatenate([row[:, None] * inv, col[:, None] * inv], axis=-1)
    cos, sin = jnp.cos(ang), jnp.sin(ang)
    reps = LANES // d_rot
    cos_t = jnp.tile(cos, (1, 2 * reps))
    sin_t = jnp.tile(jnp.concatenate([-sin, sin], axis=-1), (1, reps))
    cos_t = jnp.concatenate([cos_t, jnp.ones((pad_rows, LANES), F32)], axis=0)
    sin_t = jnp.concatenate([sin_t, jnp.zeros((pad_rows, LANES), F32)], axis=0)
    return cos_t, sin_t


def _qkv_kernel(*refs, n_q_blocks, n_rope_blocks, qscale, n_lat_blocks):
    i = pl.program_id(0)
    j = pl.program_id(1)
    if n_lat_blocks is None:
        a_ref, w_ref, cos_ref, sin_ref, o_ref = refs
    else:
        xl_ref, xc_ref, g_ref, sh_ref, sc_ref, w_ref, cos_ref, sin_ref, o_ref, a_ref = refs

        @pl.when(j == 0)
        def _():
            def mod_rows(r, carry):
                rows = pl.ds(pl.multiple_of(r * MOD_CHUNK_ROWS, MOD_CHUNK_ROWS), MOD_CHUNK_ROWS)
                x = jnp.where(i < n_lat_blocks, xl_ref[rows, :], xc_ref[rows, :])
                a_ref[rows, :] = _modulate_val(x, g_ref[...], sh_ref[0], sc_ref[0]).astype(a_ref.dtype)
                return carry

            lax.fori_loop(0, a_ref.shape[0] // MOD_CHUNK_ROWS, mod_rows, 0)
    acc = _dot(a_ref[...], w_ref[...])

    @pl.when(j >= n_rope_blocks)
    def _():
        o_ref[...] = acc.astype(o_ref.dtype)

    @pl.when(j < n_rope_blocks)
    def _():
        scale = jnp.where(j < n_q_blocks, qscale, 1.0).astype(F32)
        cos, sin = cos_ref[...], sin_ref[...]
        for g in range(acc.shape[1] // LANES):
            sl = slice(g * LANES, (g + 1) * LANES)
            o_ref[:, sl] = (_rope128(acc[:, sl], cos, sin) * scale).astype(o_ref.dtype)


def _qkv_proj(dm, h, w, cos_t, sin_t, qd, kvd):
    tm = dm.tm
    n = w.shape[1]
    tn = math.gcd(512, kvd)
    fused = isinstance(h, tuple)
    n_lat = dm.t_lat // tm
    kern = functools.partial(_qkv_kernel, n_q_blocks=qd // tn, n_rope_blocks=(qd + kvd) // tn,
                             qscale=A_HEAD_DIM ** -0.5 * LOG2E,
                             n_lat_blocks=n_lat if fused else None)
    tail_specs = [
        pl.BlockSpec((dm.d, tn), lambda i, j: (0, j)),
        pl.BlockSpec((tm, LANES), lambda i, j: (dm.ropeblk(i, tm), 0)),
        pl.BlockSpec((tm, LANES), lambda i, j: (dm.ropeblk(i, tm), 0)),
    ]
    if fused:
        x_lat, x_ctx, gain, mod3 = h
        in_specs = [
            pl.BlockSpec((tm, dm.d), lambda i, j: (jnp.minimum(i, n_lat - 1), 0)),
            pl.BlockSpec((tm, dm.d), lambda i, j: (jnp.maximum(i - n_lat, 0), 0)),
            pl.BlockSpec((1, dm.d), lambda i, j: (0, 0)),
            pl.BlockSpec((1, 1, dm.d), lambda i, j: (dm.modrow(i, tm), 0, 0)),
            pl.BlockSpec((1, 1, dm.d), lambda i, j: (dm.modrow(i, tm), 0, 1)),
        ] + tail_specs
        args = (x_lat, x_ctx, gain.reshape(1, dm.d), mod3, mod3, w, cos_t, sin_t)
        scratch = [pltpu.VMEM((tm, dm.d), BF16)]
    else:
        in_specs = [pl.BlockSpec((tm, dm.d), lambda i, j: (i, 0))] + tail_specs
        args = (h, w, cos_t, sin_t)
        scratch = []
    return pl.pallas_call(
        kern,
        grid=(dm.t_all // tm, n // tn),
        in_specs=in_specs,
        out_specs=pl.BlockSpec((tm, tn), lambda i, j: (i, j)),
        out_shape=jax.ShapeDtypeStruct((dm.t_all, n), BF16),
        scratch_shapes=scratch,
        compiler_params=_cparams("arbitrary", "arbitrary"),
        name="qkv_proj",
    )(*args)


def _attn_a_kernel(sink_ref, q_ref, kp_ref, kc_ref, kn_ref, vp_ref, vc_ref, vn_ref, kx_ref, vx_ref, o_ref,
                   *, nb, group):
    i = pl.program_id(1)
    hp = pl.program_id(2)
    npair = group // 2
    w = WINDOW
    hd = A_HEAD_DIM
    n_loc = 3 * w
    nq = npair * w
    lane = lax.broadcasted_iota(jnp.int32, (1, LANES), 1)
    lo_mask = lane < hd
    lane_q = lax.broadcasted_iota(jnp.int32, (1, nq), 1)

    k_all = jnp.concatenate([kp_ref[...], kc_ref[...], kn_ref[...], kx_ref[...]], axis=0).astype(F32)
    v_all = jnp.concatenate([vp_ref[...], vc_ref[...], vn_ref[...], vx_ref[...]], axis=0).astype(F32)
    v_t = v_all.T
    zeros_half = jnp.zeros((hd, v_t.shape[1]), F32)

    c = lax.broadcasted_iota(jnp.int32, (n_loc, w), 0)
    r = lax.broadcasted_iota(jnp.int32, (n_loc, w), 1)
    rel = c - w - r
    valid = ((rel >= -w) & (rel <= w) & ((c >= w) | (i > 0)) & ((c < 2 * w) | (i < nb - 1)) & (i < nb))
    bias = jnp.where(valid, 0.0, NEG_INF).astype(F32)
    bias = jnp.concatenate([bias] * npair, axis=1)

    scores, values = [], []
    for par in range(2):
        if par == 0:
            k_left = jnp.where(lo_mask, k_all, 0.0)
            k_right = pltpu.roll(k_left, hd, 1)
        else:
            k_right = jnp.where(lo_mask, 0.0, k_all)
            k_left = pltpu.roll(k_right, hd, 1)
        v_h = v_t[par * hd:(par + 1) * hd]
        values.append((jnp.concatenate([v_h, zeros_half], axis=0).astype(BF16),
                       jnp.concatenate([zeros_half, v_h], axis=0).astype(BF16)))
        base = par * group * hd
        qs = jnp.concatenate([q_ref[:, base + pp * LANES: base + (pp + 1) * LANES] for pp in range(npair)], axis=0)
        qs_t = qs.T
        scores.append((_dot(k_left.astype(BF16), qs_t), _dot(k_right.astype(BF16), qs_t)))

    for par in range(2):
        base = par * group * hd
        o_pair_t = None
        for side in range(2):
            s = scores[par][side]
            head0 = (2 * hp + par) * group + side
            sink = jnp.full((1, nq), sink_ref[head0], F32)
            for pp in range(1, npair):
                sink = jnp.where(lane_q >= pp * w, sink_ref[head0 + 2 * pp], sink)
            sink = sink * LOG2E
            sl = s[:n_loc] + bias
            sc = s[n_loc:]
            m = jnp.maximum(jnp.maximum(jnp.max(sl, axis=0, keepdims=True), jnp.max(sc, axis=0, keepdims=True)),
                            sink)
            el = jnp.exp2(sl - m)
            ec = jnp.exp2(sc - m)
            den = jnp.sum(el, axis=0, keepdims=True) + jnp.sum(ec, axis=0, keepdims=True) + jnp.exp2(sink - m)
            e = jnp.concatenate([el, ec], axis=0).astype(BF16)
            o_t = _dot(values[par][side], e) / den
            o_pair_t = o_t if o_pair_t is None else o_pair_t + o_t
        o_pair = o_pair_t.T
        for pp in range(npair):
            o_ref[:, base + pp * LANES: base + (pp + 1) * LANES] = o_pair[pp * w:(pp + 1) * w].astype(o_ref.dtype)


def _attn_a(dm, z, sink, n_heads, n_kv):
    group = n_heads // n_kv
    qd, kvd = n_heads * A_HEAD_DIM, n_kv * A_HEAD_DIM
    w = WINDOW
    nb = dm.s // w
    ncb = dm.c // w
    lat_blocks = dm.t_lat // w
    qw = 2 * group * A_HEAD_DIM
    kcol, vcol = qd // LANES, (qd + kvd) // LANES

    def qrow(b, i):
        return jnp.where(i < nb, b * nb + i, lat_blocks + b * ncb + (i - nb))

    def krow(b, i, off):
        return b * nb + jnp.clip(jnp.minimum(i, nb - 1) + off, 0, nb - 1)

    def kv_spec(col, off):
        return pl.BlockSpec((w, LANES), lambda b, i, hp, s_: (krow(b, i, off), col + hp))

    def ctx_spec(col):
        return pl.BlockSpec((dm.c, LANES), lambda b, i, hp, s_: (dm.t_lat // dm.c + b, col + hp))

    grid_spec = pltpu.PrefetchScalarGridSpec(
        num_scalar_prefetch=1,
        grid=(dm.b, nb + ncb, n_kv // 2),
        in_specs=[
            pl.BlockSpec((w, qw), lambda b, i, hp, s_: (qrow(b, i), hp)),
            kv_spec(kcol, -1), kv_spec(kcol, 0), kv_spec(kcol, 1),
            kv_spec(vcol, -1), kv_spec(vcol, 0), kv_spec(vcol, 1),
            ctx_spec(kcol), ctx_spec(vcol),
        ],
        out_specs=pl.BlockSpec((w, qw), lambda b, i, hp, s_: (qrow(b, i), hp)),
    )
    return pl.pallas_call(
        functools.partial(_attn_a_kernel, nb=nb, group=group),
        grid_spec=grid_spec,
        out_shape=jax.ShapeDtypeStruct((dm.t_all, qd), BF16),
        compiler_params=_cparams("arbitrary", "arbitrary", "arbitrary"),
        name="attn_window",
    )(sink, z, z, z, z, z, z, z, z, z)


def _mm_res_kernel(a_ref, w_ref, *refs, n_lat_blocks):
    if n_lat_blocks is None:
        res_ref, gate_ref, o_ref = refs
        res = res_ref[...]
    else:
        rl_ref, rc_ref, gate_ref, o_ref = refs
        res = jnp.where(pl.program_id(0) < n_lat_blocks, rl_ref[...], rc_ref[...])
    o_ref[...] = res + gate_ref[0] * _dot(a_ref[...], w_ref[...])


def _mm_res(dm, a, w, res, mod3, gate_col, rows):
    tm = 2 * dm.tm if (rows % (2 * dm.tm) == 0 and dm.s % (2 * dm.tm) == 0) else dm.tm
    k, n = w.shape
    tn = min(512, n)
    split = isinstance(res, tuple)
    n_lat = dm.t_lat // tm
    if split:
        res_specs = [pl.BlockSpec((tm, tn), lambda i, j: (jnp.minimum(i, n_lat - 1), j)),
                     pl.BlockSpec((tm, tn), lambda i, j: (jnp.maximum(i - n_lat, 0), j))]
        res_args = res
    else:
        res_specs = [pl.BlockSpec((tm, tn), lambda i, j: (i, j))]
        res_args = (res,)
    return pl.pallas_call(
        functools.partial(_mm_res_kernel, n_lat_blocks=n_lat if split else None),
        grid=(rows // tm, n // tn),
        in_specs=[
            pl.BlockSpec((tm, k), lambda i, j: (i, 0)),
            pl.BlockSpec((k, tn), lambda i, j: (0, j)),
            *res_specs,
            dm.mod_spec(gate_col, tm, tn),
        ],
        out_specs=pl.BlockSpec((tm, tn), lambda i, j: (i, j)),
        out_shape=jax.ShapeDtypeStruct((rows, n), F32),
        compiler_params=_cparams("arbitrary", "arbitrary"),
        name="out_proj_residual",
    )(a, w, *res_args, mod3)


def _mod_router_kernel(x_ref, g_ref, sh_ref, sc_ref, wh_ref, wl_ref, br_ref, h_ref, route_ref, *, n_groups, n_experts):
    h = _modulate_val(x_ref[...], g_ref[...], sh_ref[0], sc_ref[0])
    h_ref[...] = h
    h_hi, h_lo = _split_bf16(h)
    wh = wh_ref[...]
    lg = _dot(h_hi, wh) + _dot(h_lo, wh) + _dot(h_hi, wl_ref[...]) + br_ref[...]

    per = n_experts // n_groups
    lane = lax.broadcasted_iota(jnp.int32, lg.shape, 1)
    lane_f = lane.astype(F32)
    big = float(LANES)

    def first_argmax(v, vmax):
        return jnp.min(jnp.where(v == vmax, lane_f, big), axis=1, keepdims=True)

    is_g = lane < n_groups
    g_l = jnp.where(is_g, lg, NEG_INF)
    gmax = jnp.max(g_l, axis=1, keepdims=True)
    grp = first_argmax(g_l, gmax)
    p_grp = 1.0 / jnp.sum(jnp.where(is_g, jnp.exp(g_l - gmax), 0.0), axis=1, keepdims=True)

    lo = n_groups + grp * per
    e_l = jnp.where((lane_f >= lo) & (lane_f < lo + per), lg, NEG_INF)
    t1 = jnp.max(e_l, axis=1, keepdims=True)
    i1 = first_argmax(e_l, t1)
    e_l2 = jnp.where(lane_f == i1, NEG_INF, e_l)
    t2 = jnp.max(e_l2, axis=1, keepdims=True)
    i2 = first_argmax(e_l2, t2)
    ratio = jnp.exp(t2 - t1)
    gate1 = p_grp / (1.0 + ratio)
    gate2 = gate1 * ratio
    route = jnp.where(lane == 0, i1 - n_groups,
                      jnp.where(lane == 1, i2 - n_groups,
                                jnp.where(lane == 2, gate1, jnp.where(lane == 3, gate2, 0.0))))
    route_ref[...] = route


def _mod_router(dm, x, g, mod3, shift_col, scale_col, w_r, b_r, rows, n_groups, n_experts):
    tm = min(dm.tm, 256)
    w_hi, w_lo = _split_bf16(w_r)
    kern = functools.partial(_mod_router_kernel, n_groups=n_groups, n_experts=n_experts)
    return pl.pallas_call(
        kern,
        grid=(rows // tm,),
        in_specs=[
            pl.BlockSpec((tm, dm.d), lambda i: (i, 0)),
            pl.BlockSpec((1, dm.d), lambda i: (0, 0)),
            dm.mod_spec(shift_col, tm),
            dm.mod_spec(scale_col, tm),
            pl.BlockSpec((dm.d, LANES), lambda i: (0, 0)),
            pl.BlockSpec((dm.d, LANES), lambda i: (0, 0)),
            pl.BlockSpec((1, LANES), lambda i: (0, 0)),
        ],
        out_specs=[
            pl.BlockSpec((tm, dm.d), lambda i: (i, 0)),
            pl.BlockSpec((tm, LANES), lambda i: (i, 0)),
        ],
        out_shape=[
            jax.ShapeDtypeStruct((rows, dm.d), F32),
            jax.ShapeDtypeStruct((rows, LANES), F32),
        ],
        compiler_params=_cparams("arbitrary"),
        name="modulate_router",
    )(x, g.reshape(1, dm.d), mod3, mod3, w_hi, w_lo, b_r)


def _dispatch(route, n_experts):
    eid = route[:, :2].astype(jnp.int32).reshape(-1)
    n_assign = eid.shape[0]
    order = jnp.argsort(eid).astype(jnp.int32)
    rank = jnp.argsort(order).astype(jnp.int32)
    onehot = eid[:, None] == jnp.arange(n_experts, dtype=jnp.int32)[None, :]
    counts = jnp.sum(onehot, axis=0, dtype=jnp.int32)
    padded = (counts + MOE_BLOCK - 1) // MOE_BLOCK * MOE_BLOCK
    p_end = jnp.cumsum(padded)
    p_start = p_end - padded
    c_start = jnp.cumsum(counts) - counts
    dest = rank + jnp.sum(jnp.where(onehot, (p_start - c_start)[None, :], 0), axis=1, dtype=jnp.int32)
    n_rows = (n_assign + n_experts * (MOE_BLOCK - 1) + MOE_BLOCK - 1) // MOE_BLOCK * MOE_BLOCK
    n_blocks = n_rows // MOE_BLOCK
    blk_start = jnp.arange(n_blocks, dtype=jnp.int32) * MOE_BLOCK
    blk_e = jnp.minimum(jnp.sum(p_end[None, :] <= blk_start[:, None], axis=1, dtype=jnp.int32), n_experts - 1)
    off = (blk_start - p_start[blk_e])[:, None] + jnp.arange(MOE_BLOCK, dtype=jnp.int32)[None, :]
    pos = jnp.clip(c_start[blk_e][:, None] + off, 0, n_assign - 1)
    src_tok = jnp.where(off < counts[blk_e][:, None], order[pos] // 2, 0).reshape(n_rows).astype(jnp.int32)
    n_used = (p_end[-1:] // MOE_BLOCK).astype(jnp.int32)
    ids = jnp.arange(n_experts, dtype=jnp.int32)
    later = (ids[None, :] > ids[:, None]) & (counts[None, :] > 0)
    nxt = jnp.min(jnp.where(later, ids[None, :], n_experts), axis=1)
    nxt_e = jnp.where(nxt < n_experts, nxt, -1).astype(jnp.int32)[blk_e]
    return blk_e, nxt_e, src_tok, n_used, dest, n_rows


def _moe_kernel(blk_e_ref, nxt_e_ref, src_ref, nused_ref, h_hbm, wgu_hbm, wdn_hbm, y_ref,
                xb, sem, wst_gu, wst_dn, wb_gu, wb_dn, wsem, *, d_exp, layer):
    b = pl.program_id(0)
    n_used = nused_ref[0]
    slot = b % 2
    e_cur = blk_e_ref[b]
    run_start = jnp.logical_or(b == 0, e_cur != blk_e_ref[jnp.maximum(b - 1, 0)])

    def row_copies(blk, slot_):
        return [pltpu.make_async_copy(h_hbm.at[pl.ds(src_ref[blk * MOE_BLOCK + r], 1), :],
                                      xb.at[slot_, pl.ds(r, 1), :], sem.at[slot_])
                for r in range(MOE_BLOCK)]

    def weight_copies(e):
        return (pltpu.make_async_copy(wgu_hbm.at[layer, e], wst_gu, wsem.at[0]),
                pltpu.make_async_copy(wdn_hbm.at[layer, e], wst_dn, wsem.at[1]))

    @pl.when(b == 0)
    def _():
        for cp in row_copies(0, 0):
            cp.start()
        for cp in weight_copies(e_cur):
            cp.start()

    @pl.when(b >= n_used)
    def _():
        y_ref[...] = jnp.zeros(y_ref.shape, y_ref.dtype)

    @pl.when(b < n_used)
    def _():
        @pl.when(run_start)
        def _():
            for cp in weight_copies(e_cur):
                cp.wait()
            rows_gu = wst_gu.shape[0] // WEIGHT_CAST_CHUNKS
            rows_dn = wst_dn.shape[0] // WEIGHT_CAST_CHUNKS

            def cast_chunk(ci, carry):
                r0 = pl.multiple_of(ci * rows_gu, rows_gu)
                wb_gu[pl.ds(r0, rows_gu), :] = wst_gu[pl.ds(r0, rows_gu), :].astype(BF16)
                r1 = pl.multiple_of(ci * rows_dn, rows_dn)
                wb_dn[pl.ds(r1, rows_dn), :] = wst_dn[pl.ds(r1, rows_dn), :].astype(BF16)
                return carry

            lax.fori_loop(0, WEIGHT_CAST_CHUNKS, cast_chunk, 0)
            nxt = nxt_e_ref[b]

            @pl.when(nxt >= 0)
            def _():
                for cp in weight_copies(nxt):
                    cp.start()

        for cp in row_copies(b, slot):
            cp.wait()

        @pl.when(b + 1 < n_used)
        def _():
            for cp in row_copies(b + 1, 1 - slot):
                cp.start()

        xrow = xb[slot].astype(BF16)
        gu = _dot(xrow, wb_gu[...])
        g, u = gu[:, :d_exp], gu[:, d_exp:]
        act = (g * jax.nn.sigmoid(g) * u).astype(BF16)
        y_ref[...] = _dot(act, wb_dn[...])


def _moe_experts(h, blk_e, nxt_e, src_tok, n_used, w_gu, w_dn, layer, n_rows):
    _, _, d, f2 = w_gu.shape
    d_exp = f2 // 2
    n_blocks = n_rows // MOE_BLOCK
    grid_spec = pltpu.PrefetchScalarGridSpec(
        num_scalar_prefetch=4,
        grid=(n_blocks,),
        in_specs=[
            pl.BlockSpec(memory_space=pl.ANY),
            pl.BlockSpec(memory_space=pl.ANY),
            pl.BlockSpec(memory_space=pl.ANY),
        ],
        out_specs=pl.BlockSpec((MOE_BLOCK, d), lambda b, be, ne, st, nu: (b, 0)),
        scratch_shapes=[
            pltpu.VMEM((2, MOE_BLOCK, d), F32),
            pltpu.SemaphoreType.DMA((2,)),
            pltpu.VMEM((d, f2), F32),
            pltpu.VMEM((d_exp, d), F32),
            pltpu.VMEM((d, f2), BF16),
            pltpu.VMEM((d_exp, d), BF16),
            pltpu.SemaphoreType.DMA((2,)),
        ],
    )
    return pl.pallas_call(
        functools.partial(_moe_kernel, d_exp=d_exp, layer=layer),
        grid_spec=grid_spec,
        out_shape=jax.ShapeDtypeStruct((n_rows, d), F32),
        compiler_params=_cparams("arbitrary"),
        name="moe_experts",
    )(blk_e, nxt_e, src_tok, n_used, h, w_gu, w_dn)


def _combine_kernel(dest_ref, y_hbm, x_ref, route_ref, g2_ref, *refs, tc, n_blocks, final):
    if final:
        fg_ref, out_ref, yb, sem = refs
    else:
        ng_ref, sh_ref, sc_ref, xo_ref, h_ref, yb, sem = refs
    t = pl.program_id(0)
    slot = t % 2

    def row_copies(blk, slot_):
        return [pltpu.make_async_copy(y_hbm.at[pl.ds(dest_ref[(blk * tc + r) * 2 + k], 1), :],
                                      yb.at[slot_, k, pl.ds(r, 1), :], sem.at[slot_])
                for r in range(tc) for k in range(2)]

    @pl.when(t == 0)
    def _():
        for cp in row_copies(0, 0):
            cp.start()

    for cp in row_copies(t, slot):
        cp.wait()

    @pl.when(t + 1 < n_blocks)
    def _():
        for cp in row_copies(t + 1, 1 - slot):
            cp.start()

    route = route_ref[...]
    y = route[:, 2:3] * yb[slot, 0] + route[:, 3:4] * yb[slot, 1]
    xn = x_ref[...] + g2_ref[0] * y
    if final:
        out_ref[...] = _rms(xn) * fg_ref[...]
    else:
        xo_ref[...] = xn
        h_ref[...] = _modulate_val(xn, ng_ref[...], sh_ref[0], sc_ref[0]).astype(h_ref.dtype)


def _combine(dm, y_buf, dest, x, route, mod3, gate_col, rows, *, final_g=None, next_g=None, next_mod3=None):
    tc = MOE_BLOCK
    n_blocks = rows // tc
    final = final_g is not None
    row_spec = pl.BlockSpec((tc, dm.d), lambda t, d_: (t, 0))
    vec_spec = pl.BlockSpec((1, dm.d), lambda t, d_: (0, 0))
    in_specs = [
        pl.BlockSpec(memory_space=pl.ANY),
        row_spec,
        pl.BlockSpec((tc, LANES), lambda t, d_: (t, 0)),
        dm.mod_spec(gate_col, tc),
    ]
    if final:
        in_specs += [vec_spec]
        args = (final_g.reshape(1, dm.d),)
        out_specs = row_spec
        out_shape = jax.ShapeDtypeStruct((rows, dm.d), F32)
    else:
        in_specs += [vec_spec, dm.mod_spec(0, tc), dm.mod_spec(1, tc)]
        args = (next_g.reshape(1, dm.d), next_mod3, next_mod3)
        out_specs = [row_spec, row_spec]
        out_shape = [jax.ShapeDtypeStruct((rows, dm.d), F32), jax.ShapeDtypeStruct((rows, dm.d), BF16)]
    grid_spec = pltpu.PrefetchScalarGridSpec(
        num_scalar_prefetch=1,
        grid=(n_blocks,),
        in_specs=in_specs,
        out_specs=out_specs,
        scratch_shapes=[
            pltpu.VMEM((2, 2, tc, dm.d), F32),
            pltpu.SemaphoreType.DMA((2,)),
        ],
    )
    return pl.pallas_call(
        functools.partial(_combine_kernel, tc=tc, n_blocks=n_blocks, final=final),
        grid_spec=grid_spec,
        out_shape=out_shape,
        compiler_params=_cparams("arbitrary"),
        name="moe_combine",
    )(dest, y_buf, x, route, mod3, *args)


def _dkv_kernel(a_ref, wq_ref, wkv_ref, gq_ref, gkv_ref, cos_ref, sin_ref, zq_ref, ckv_ref, kr_ref, *, kv_rank):
    a = a_ref[...]
    zq_ref[...] = (_rms(_dot(a, wq_ref[...])) * gq_ref[...]).astype(zq_ref.dtype)
    zkv = _dot(a, wkv_ref[...])
    ckv_ref[...] = (_rms(zkv[:, :kv_rank]) * gkv_ref[...]).astype(ckv_ref.dtype)
    kr_ref[...] = _rope128(zkv[:, kv_rank:], cos_ref[...], sin_ref[...]).astype(kr_ref.dtype)


def _dkv_proj(dm, h, wq, wkv, gq, gkv, cos_t, sin_t):
    tm = dm.tm
    q_rank = wq.shape[1]
    kv_rank = wkv.shape[1] - LANES
    rope_spec = pl.BlockSpec((tm, LANES), lambda i: (dm.ropeblk(i, tm), 0))
    return pl.pallas_call(
        functools.partial(_dkv_kernel, kv_rank=kv_rank),
        grid=(dm.t_all // tm,),
        in_specs=[
            pl.BlockSpec((tm, dm.d), lambda i: (i, 0)),
            pl.BlockSpec((dm.d, q_rank), lambda i: (0, 0)),
            pl.BlockSpec((dm.d, kv_rank + LANES), lambda i: (0, 0)),
            pl.BlockSpec((1, q_rank), lambda i: (0, 0)),
            pl.BlockSpec((1, kv_rank), lambda i: (0, 0)),
            rope_spec, rope_spec,
        ],
        out_specs=[
            pl.BlockSpec((tm, q_rank), lambda i: (i, 0)),
            pl.BlockSpec((tm, kv_rank), lambda i: (i, 0)),
            pl.BlockSpec((tm, LANES), lambda i: (i, 0)),
        ],
        out_shape=[
            jax.ShapeDtypeStruct((dm.t_all, q_rank), BF16),
            jax.ShapeDtypeStruct((dm.t_all, kv_rank), BF16),
            jax.ShapeDtypeStruct((dm.t_all, LANES), BF16),
        ],
        compiler_params=_cparams("arbitrary"),
        name="mla_down_proj",
    )(h, wq, wkv, gq.reshape(1, -1), gkv.reshape(1, -1), cos_t, sin_t)


def _qup_kernel(a_ref, w_ref, cos_ref, sin_ref, o_ref, *, qscale):
    acc = _dot(a_ref[...], w_ref[...])
    cos, sin = cos_ref[...], sin_ref[...]
    for g in range(acc.shape[1] // LANES):
        sl = slice(g * LANES, (g + 1) * LANES)
        x = acc[:, sl]
        if g % 2 == 1:
            x = _rope128(x, cos, sin)
        o_ref[:, sl] = (x * qscale).astype(o_ref.dtype)


def _qup_proj(dm, zq, w, cos_t, sin_t, qscale):
    tm = dm.tm
    k, n = w.shape
    tn = min(1024, n)
    rope_spec = pl.BlockSpec((tm, LANES), lambda i, j: (dm.ropeblk(i, tm), 0))
    return pl.pallas_call(
        functools.partial(_qup_kernel, qscale=qscale),
        grid=(dm.t_lat // tm, n // tn),
        in_specs=[
            pl.BlockSpec((tm, k), lambda i, j: (i, 0)),
            pl.BlockSpec((k, tn), lambda i, j: (0, j)),
            rope_spec, rope_spec,
        ],
        out_specs=pl.BlockSpec((tm, tn), lambda i, j: (i, j)),
        out_shape=jax.ShapeDtypeStruct((dm.t_lat, n), BF16),
        compiler_params=_cparams("arbitrary", "arbitrary"),
        name="mla_q_up",
    )(zq, w, cos_t, sin_t)


def _kvup_kernel(a_ref, wk_ref, wvt_ref, kr_ref, k_ref, vt_ref, *, heads):
    a = a_ref[...]
    kn = _dot(a, wk_ref[...])
    kr = kr_ref[...]
    for h in range(heads):
        k_ref[:, h * 2 * LANES: h * 2 * LANES + LANES] = kn[:, h * LANES: (h + 1) * LANES].astype(k_ref.dtype)
        k_ref[:, h * 2 * LANES + LANES: (h + 1) * 2 * LANES] = kr
    vt_ref[...] = _dot_nt(wvt_ref[...], a).astype(vt_ref.dtype)


def _kvup_proj(dm, ckv, w_uk, w_uvt, kr, n_heads):
    tm = dm.tm
    kv_rank = ckv.shape[1]
    hb = math.gcd(4, n_heads)
    return pl.pallas_call(
        functools.partial(_kvup_kernel, heads=hb),
        grid=(dm.t_all // tm, n_heads // hb),
        in_specs=[
            pl.BlockSpec((tm, kv_rank), lambda i, j: (i, 0)),
            pl.BlockSpec((kv_rank, hb * LANES), lambda i, j: (0, j)),
            pl.BlockSpec((hb * LANES, kv_rank), lambda i, j: (j, 0)),
            pl.BlockSpec((tm, LANES), lambda i, j: (i, 0)),
        ],
        out_specs=[
            pl.BlockSpec((tm, hb * 2 * LANES), lambda i, j: (i, j)),
            pl.BlockSpec((hb * LANES, tm), lambda i, j: (j, i)),
        ],
        out_shape=[
            jax.ShapeDtypeStruct((dm.t_all, n_heads * 2 * LANES), BF16),
            jax.ShapeDtypeStruct((n_heads * LANES, dm.t_all), BF16),
        ],
        compiler_params=_cparams("arbitrary", "arbitrary"),
        name="mla_kv_up",
    )(ckv, w_uk, w_uvt, kr)


def _mla_flash_kernel(q_ref, kl_ref, kc_ref, vl_ref, vc_ref, o_ref, m_sc, l_sc, acc_sc, sa_sc, sb_sc, pa_sc, pb_sc,
                      qt_sc, *, tk, n_chunks):
    tq = q_ref.shape[0]
    half = tq // 2
    m_sc[...] = jnp.full(m_sc.shape, NEG_INF, F32)
    l_sc[...] = jnp.zeros(l_sc.shape, F32)
    acc_sc[...] = jnp.zeros(acc_sc.shape, F32)
    pb_sc[...] = jnp.zeros(pb_sc.shape, pb_sc.dtype)
    qt_sc[...] = q_ref[...].T

    def k_chunk(ci):
        return kl_ref[pl.ds(pl.multiple_of(ci * tk, tk), tk), :]

    def vt_chunk(ci):
        return vl_ref[:, pl.ds(pl.multiple_of(ci * tk, tk), tk)]

    def softmax(s, sl):
        m_prev = m_sc[:, sl]
        m_new = jnp.maximum(m_prev, jnp.max(s, axis=0, keepdims=True))
        alpha = jnp.exp2(m_prev - m_new)
        p = jnp.exp2(s - m_new)
        l_sc[:, sl] = alpha * l_sc[:, sl] + jnp.sum(p, axis=0, keepdims=True)
        m_sc[:, sl] = m_new
        return alpha, p.astype(BF16)

    def stage(c_prev, c_next, s_cur, s_next, p_prev, p_cur):
        vt = vt_chunk(c_prev)
        kn = k_chunk(c_next)
        for hf in range(2):
            sl = slice(hf * half, (hf + 1) * half)
            pv = _dot(vt, p_prev[:, sl])
            s_next[:, sl] = _dot(kn, qt_sc[:, sl])
            alpha, p = softmax(s_cur[:, sl], sl)
            p_cur[:, sl] = p
            acc_sc[:, sl] = alpha * (acc_sc[:, sl] + pv)

    sa_sc[...] = _dot(k_chunk(0), qt_sc[...])

    def body(j, carry):
        c0 = 2 * j
        stage(jnp.maximum(c0 - 1, 0), c0 + 1, sa_sc, sb_sc, pb_sc, pa_sc)
        stage(c0, jnp.minimum(c0 + 2, n_chunks - 1), sb_sc, sa_sc, pa_sc, pb_sc)
        return carry

    lax.fori_loop(0, n_chunks // 2, body, 0)
    pv = _dot(vt_chunk(n_chunks - 1), pb_sc[...])
    alpha, p = softmax(_dot(kc_ref[...], qt_sc[...]), slice(None))
    acc = alpha * (acc_sc[...] + pv) + _dot(vc_ref[...], p)
    o_ref[...] = (acc / l_sc[...]).T.astype(o_ref.dtype)


def _mla_flash(dm, q, k, vt, n_heads):
    tq = min(512, dm.s)
    tk = min(512, dm.s // 2)
    nq = dm.s // tq
    n_chunks = dm.s // tk
    assert n_chunks % 2 == 0 and tq % (2 * LANES) == 0
    return pl.pallas_call(
        functools.partial(_mla_flash_kernel, tk=tk, n_chunks=n_chunks),
        grid=(dm.b, n_heads, nq),
        in_specs=[
            pl.BlockSpec((tq, 2 * LANES), lambda b, h, i: (b * nq + i, h)),
            pl.BlockSpec((dm.s, 2 * LANES), lambda b, h, i: (b, h)),
            pl.BlockSpec((dm.c, 2 * LANES), lambda b, h, i: (dm.t_lat // dm.c + b, h)),
            pl.BlockSpec((LANES, dm.s), lambda b, h, i: (h, b)),
            pl.BlockSpec((LANES, dm.c), lambda b, h, i: (h, dm.t_lat // dm.c + b)),
        ],
        out_specs=pl.BlockSpec((tq, LANES), lambda b, h, i: (b * nq + i, h)),
        out_shape=jax.ShapeDtypeStruct((dm.t_lat, n_heads * LANES), BF16),
        scratch_shapes=[
            pltpu.VMEM((1, tq), F32),
            pltpu.VMEM((1, tq), F32),
            pltpu.VMEM((LANES, tq), F32),
            pltpu.VMEM((tk, tq), F32),
            pltpu.VMEM((tk, tq), F32),
            pltpu.VMEM((tk, tq), BF16),
            pltpu.VMEM((tk, tq), BF16),
            pltpu.VMEM((2 * LANES, tq), BF16),
        ],
        compiler_params=_cparams("arbitrary", "arbitrary", "arbitrary"),
        name="mla_flash",
    )(q, k, k, vt, vt)


def _mixer_a(dm, h, w_qkv, sink, cos_t, sin_t):
    n_heads = sink.shape[0]
    n_kv = (w_qkv.shape[1] // A_HEAD_DIM - n_heads) // 2
    assert (n_heads // n_kv) % 2 == 0 and n_kv % 2 == 0
    z = _qkv_proj(dm, h, w_qkv.astype(BF16), cos_t, sin_t, n_heads * A_HEAD_DIM, n_kv * A_HEAD_DIM)
    return _attn_a(dm, z, sink, n_heads, n_kv)


def _mixer_b(dm, h, w_dkv, g_q, g_kv, w_uq, w_ukv, wo_rows, cos_t, sin_t):
    q_rank, kv_rank = g_q.shape[0], g_kv.shape[0]
    rope = w_dkv.shape[1] - q_rank - kv_rank
    n_heads = (w_uq.shape[1] - w_ukv.shape[1] + wo_rows) // rope
    nope = (w_ukv.shape[1] - wo_rows) // n_heads
    d_v = wo_rows // n_heads
    assert rope == A_HEAD_DIM and nope == LANES and d_v == LANES
    wq = w_dkv[:, :q_rank].astype(BF16)
    wkv = jnp.pad(w_dkv[:, q_rank:], ((0, 0), (0, LANES - rope))).astype(BF16)
    zq, ckv, kr = _dkv_proj(dm, h, wq, wkv, g_q, g_kv, cos_t, sin_t)
    w_uq_p = jnp.pad(w_uq.reshape(q_rank, n_heads, nope + rope), ((0, 0), (0, 0), (0, LANES - rope)))
    w_uq_p = w_uq_p.reshape(q_rank, n_heads * 2 * LANES).astype(BF16)
    qscale = (nope + rope) ** -0.5 * math.log2(math.e)
    q = _qup_proj(dm, zq, w_uq_p, cos_t, sin_t, qscale)
    w_ukv3 = w_ukv.reshape(kv_rank, n_heads, nope + d_v)
    w_uk = w_ukv3[..., :nope].reshape(kv_rank, n_heads * nope).astype(BF16)
    w_uvt = jnp.transpose(w_ukv3[..., nope:], (1, 2, 0)).reshape(n_heads * d_v, kv_rank).astype(BF16)
    k, vt = _kvup_proj(dm, ckv, w_uk, w_uvt, kr, n_heads)
    return _mla_flash(dm, q, k, vt, n_heads)


def kernel(x, c, ctx, c_ctx, ada_w, ada_b, norm_g, final_g, a_wqkv, a_wo, a_sink, b_wdkv, b_gq, b_gkv, b_wuq,
           b_wukv, b_wo, r_wg, r_bg, r_we, r_be, e_wgu, e_wdn):
    b, s, d = x.shape
    c_len = ctx.shape[1]
    depth = ada_w.shape[0]
    dm = _Dims(b, s, c_len, d)
    assert b + 1 <= MOD_ROWS and dm.t_lat % c_len == 0 and c_len % WINDOW == 0 and s % GRID_W == 0
    n_groups, n_experts = r_wg.shape[2], r_we.shape[2]
    assert n_groups + n_experts <= LANES

    cc = jnp.concatenate([c, c_ctx[None, :], jnp.zeros((MOD_ROWS - b - 1, d), F32)], axis=0)
    mod = _ada(cc, ada_w, ada_b)
    mod3 = [mod[i].reshape(MOD_ROWS, 1, 6 * d) for i in range(depth)]
    cos_t, sin_t = _rope_tables(s, A_HEAD_DIM, dm.tm)

    xa = (x.reshape(dm.t_lat, d), ctx.reshape(dm.t_ctx, d))
    h = xa + (norm_g[0, 0], mod3[0])
    out = None
    for i in range(depth):
        last = i == depth - 1
        j = i // 2
        rows = dm.t_lat if last else dm.t_all
        if i % 2 == 0:
            o = _mixer_a(dm, h, a_wqkv[j], a_sink[j], cos_t, sin_t)
            w_o = a_wo[j]
        else:
            o = _mixer_b(dm, h, b_wdkv[j], b_gq[j], b_gkv[j], b_wuq[j], b_wukv[j], b_wo.shape[1], cos_t, sin_t)
            w_o = b_wo[j]
        x1 = _mm_res(dm, o, w_o.astype(BF16), xa, mod3[i], 2, rows)
        w_r = jnp.pad(jnp.concatenate([r_wg[i], r_we[i]], axis=1), ((0, 0), (0, LANES - n_groups - n_experts)))
        b_r = jnp.pad(jnp.concatenate([r_bg[i], r_be[i]]), (0, LANES - n_groups - n_experts)).reshape(1, LANES)
        h2, route = _mod_router(dm, x1, norm_g[i, 1], mod3[i], 3, 4, w_r, b_r, rows, n_groups, n_experts)
        blk_e, nxt_e, src_tok, n_used, dest, n_rows = _dispatch(route, n_experts)
        y_buf = _moe_experts(h2, blk_e, nxt_e, src_tok, n_used, e_wgu, e_wdn, i, n_rows)
        if last:
            out = _combine(dm, y_buf, dest, x1, route, mod3[i], 5, rows, final_g=final_g)
        else:
            xa, h = _combine(dm, y_buf, dest, x1, route, mod3[i], 5, rows,
                             next_g=norm_g[i + 1, 0], next_mod3=mod3[i + 1])
    return out.reshape(b, s, d)
```

```python
import functools
import math

import jax
import jax.numpy as jnp
from jax import lax
from jax.experimental import pallas as pl
from jax.experimental.pallas import tpu as pltpu

F32 = jnp.float32
BF16 = jnp.bfloat16

EPS = 1e-6
NEG_INF = -1e30
ROPE_BASE = 10000.0
GRID_W = 64
A_HEAD_DIM = 64
WINDOW = 128
MOE_BLOCK = 128
LANES = 128
MOD_ROWS = 8
LOG2E = math.log2(math.e)
WEIGHT_CAST_CHUNKS = 8
BULK_DMA_PRIORITY = 1
MOD_CHUNK_ROWS = 32
VMEM_LIMIT_BYTES = 56 * 1024 * 1024


def _cparams(*sem):
    return pltpu.CompilerParams(dimension_semantics=sem, vmem_limit_bytes=VMEM_LIMIT_BYTES)


def _dot(a, b):
    return jnp.dot(a, b, preferred_element_type=F32)


def _dot_nt(a, b):
    return lax.dot_general(a, b, (((1,), (1,)), ((), ())), preferred_element_type=F32)


def _split_bf16(v):
    hi = v.astype(BF16)
    lo = (v - hi.astype(F32)).astype(BF16)
    return hi, lo


def _rms(x):
    return x * lax.rsqrt(jnp.mean(x * x, axis=-1, keepdims=True) + EPS)


def _modulate_val(x, g, shift, scale):
    return _rms(x) * g * (1.0 + scale) + shift


def _rope128(x, cos, sin_signed):
    lane = lax.broadcasted_iota(jnp.int32, x.shape, 1)
    first_half = (lane & 32) == 0
    partner = jnp.where(first_half, pltpu.roll(x, 96, 1), pltpu.roll(x, 32, 1))
    return x * cos + partner * sin_signed


def _ada_kernel(c_ref, w_ref, b_ref, o_ref):
    c = c_ref[...]
    s = c * jax.nn.sigmoid(c)
    s_hi, s_lo = _split_bf16(s)
    w_hi, w_lo = _split_bf16(w_ref[...])
    r = _dot(jnp.concatenate([s_hi, s_lo], axis=0), w_hi)
    o_ref[...] = r[:MOD_ROWS] + r[MOD_ROWS:] + _dot(s_hi, w_lo) + b_ref[...]


def _ada(cc, ada_w, ada_b):
    depth, d, n = ada_w.shape
    tn = 512
    return pl.pallas_call(
        _ada_kernel,
        grid=(depth, n // tn),
        in_specs=[
            pl.BlockSpec((MOD_ROWS, d), lambda l, j: (0, 0)),
            pl.BlockSpec((None, d, tn), lambda l, j: (l, 0, j)),
            pl.BlockSpec((None, 1, tn), lambda l, j: (l, 0, j)),
        ],
        out_specs=pl.BlockSpec((None, MOD_ROWS, tn), lambda l, j: (l, 0, j)),
        out_shape=jax.ShapeDtypeStruct((depth, MOD_ROWS, n), F32),
        compiler_params=_cparams("arbitrary", "arbitrary"),
        name="ada",
    )(cc, ada_w, ada_b.reshape(depth, 1, n))


class _Dims:
    def __init__(self, b, s, c, d):
        self.b, self.s, self.c, self.d = b, s, c, d
        self.t_lat, self.t_ctx = b * s, b * c
        self.t_all = self.t_lat + self.t_ctx
        self.tm = next(t for t in (512, 256, 128) if s % t == 0 and self.t_ctx % t == 0)

    def modrow(self, i, tm):
        return jnp.minimum(i // (self.s // tm), self.b)

    def ropeblk(self, i, tm):
        return jnp.where(i < self.t_lat // tm, i % (self.s // tm), self.s // tm)

    def mod_spec(self, col, tm, width=None):
        width = self.d if width is None else width
        per = self.d // width
        if width == self.d:
            return pl.BlockSpec((1, 1, width), lambda i, *_: (self.modrow(i, tm), 0, col))
        return pl.BlockSpec((1, 1, width), lambda i, j, *_: (self.modrow(i, tm), 0, col * per + j))


def _rope_tables(s, d_rot, pad_rows):
    rows = s // GRID_W
    row = jnp.repeat(jnp.arange(rows, dtype=F32), GRID_W)
    col = jnp.tile(jnp.arange(GRID_W, dtype=F32), rows)
    n_freq = d_rot // 4
    inv = ROPE_BASE ** (-jnp.arange(n_freq, dtype=F32) / n_freq)
    ang = jnp.concatenate([row[:, None] * inv, col[:, None] * inv], axis=-1)
    cos, sin = jnp.cos(ang), jnp.sin(ang)
    reps = LANES // d_rot
    cos_t = jnp.tile(cos, (1, 2 * reps))
    sin_t = jnp.tile(jnp.concatenate([-sin, sin], axis=-1), (1, reps))
    cos_t = jnp.concatenate([cos_t, jnp.ones((pad_rows, LANES), F32)], axis=0)
    sin_t = jnp.concatenate([sin_t, jnp.zeros((pad_rows, LANES), F32)], axis=0)
    return cos_t, sin_t


def _qkv_kernel(*refs, n_q_blocks, n_rope_blocks, qscale, n_lat_blocks):
    i = pl.program_id(0)
    j = pl.program_id(1)
    if n_lat_blocks is None:
        a_ref, w_ref, cos_ref, sin_ref, o_ref = refs
    else:
        xl_ref, xc_ref, g_ref, sh_ref, sc_ref, w_ref, cos_ref, sin_ref, o_ref, a_ref = refs

        @pl.when(j == 0)
        def _():
            def mod_rows(r, carry):
                rows = pl.ds(pl.multiple_of(r * MOD_CHUNK_ROWS, MOD_CHUNK_ROWS), MOD_CHUNK_ROWS)
                x = jnp.where(i < n_lat_blocks, xl_ref[rows, :], xc_ref[rows, :])
                a_ref[rows, :] = _modulate_val(x, g_ref[...], sh_ref[0], sc_ref[0]).astype(a_ref.dtype)
                return carry

            lax.fori_loop(0, a_ref.shape[0] // MOD_CHUNK_ROWS, mod_rows, 0)
    acc = _dot(a_ref[...], w_ref[...])

    @pl.when(j >= n_rope_blocks)
    def _():
        o_ref[...] = acc.astype(o_ref.dtype)

    @pl.when(j < n_rope_blocks)
    def _():
        scale = jnp.where(j < n_q_blocks, qscale, 1.0).astype(F32)
        cos, sin = cos_ref[...], sin_ref[...]
        for g in range(acc.shape[1] // LANES):
            sl = slice(g * LANES, (g + 1) * LANES)
            o_ref[:, sl] = (_rope128(acc[:, sl], cos, sin) * scale).astype(o_ref.dtype)


def _qkv_proj(dm, h, w, cos_t, sin_t, qd, kvd):
    tm = dm.tm
    n = w.shape[1]
    tn = math.gcd(512, kvd)
    fused = isinstance(h, tuple)
    n_lat = dm.t_lat // tm
    kern = functools.partial(_qkv_kernel, n_q_blocks=qd // tn, n_rope_blocks=(qd + kvd) // tn,
                             qscale=A_HEAD_DIM ** -0.5 * LOG2E,
                             n_lat_blocks=n_lat if fused else None)
    tail_specs = [
        pl.BlockSpec((dm.d, tn), lambda i, j: (0, j)),
        pl.BlockSpec((tm, LANES), lambda i, j: (dm.ropeblk(i, tm), 0)),
        pl.BlockSpec((tm, LANES), lambda i, j: (dm.ropeblk(i, tm), 0)),
    ]
    if fused:
        x_lat, x_ctx, gain, mod3 = h
        in_specs = [
            pl.BlockSpec((tm, dm.d), lambda i, j: (jnp.minimum(i, n_lat - 1), 0)),
            pl.BlockSpec((tm, dm.d), lambda i, j: (jnp.maximum(i - n_lat, 0), 0)),
            pl.BlockSpec((1, dm.d), lambda i, j: (0, 0)),
            pl.BlockSpec((1, 1, dm.d), lambda i, j: (dm.modrow(i, tm), 0, 0)),
            pl.BlockSpec((1, 1, dm.d), lambda i, j: (dm.modrow(i, tm), 0, 1)),
        ] + tail_specs
        args = (x_lat, x_ctx, gain.reshape(1, dm.d), mod3, mod3, w, cos_t, sin_t)
        scratch = [pltpu.VMEM((tm, dm.d), BF16)]
    else:
        in_specs = [pl.BlockSpec((tm, dm.d), lambda i, j: (i, 0))] + tail_specs
        args = (h, w, cos_t, sin_t)
        scratch = []
    return pl.pallas_call(
        kern,
        grid=(dm.t_all // tm, n // tn),
        in_specs=in_specs,
        out_specs=pl.BlockSpec((tm, tn), lambda i, j: (i, j)),
        out_shape=jax.ShapeDtypeStruct((dm.t_all, n), BF16),
        scratch_shapes=scratch,
        compiler_params=_cparams("arbitrary", "arbitrary"),
        name="qkv_proj",
    )(*args)


def _attn_a_kernel(sink_ref, q_ref, kp_ref, kc_ref, kn_ref, vp_ref, vc_ref, vn_ref, kx_ref, vx_ref, o_ref,
                   *, nb, group):
    i = pl.program_id(1)
    hp = pl.program_id(2)
    npair = group // 2
    w = WINDOW
    hd = A_HEAD_DIM
    n_loc = 3 * w
    nq = npair * w
    lane = lax.broadcasted_iota(jnp.int32, (1, LANES), 1)
    lo_mask = lane < hd
    lane_q = lax.broadcasted_iota(jnp.int32, (1, nq), 1)

    k_all = jnp.concatenate([kp_ref[...], kc_ref[...], kn_ref[...], kx_ref[...]], axis=0).astype(F32)
    v_all = jnp.concatenate([vp_ref[...], vc_ref[...], vn_ref[...], vx_ref[...]], axis=0).astype(F32)
    v_t = v_all.T
    zeros_half = jnp.zeros((hd, v_t.shape[1]), F32)

    c = lax.broadcasted_iota(jnp.int32, (n_loc, w), 0)
    r = lax.broadcasted_iota(jnp.int32, (n_loc, w), 1)
    rel = c - w - r
    valid = ((rel >= -w) & (rel <= w) & ((c >= w) | (i > 0)) & ((c < 2 * w) | (i < nb - 1)) & (i < nb))
    bias = jnp.where(valid, 0.0, NEG_INF).astype(F32)
    bias = jnp.concatenate([bias] * npair, axis=1)

    scores, values = [], []
    for par in range(2):
        if par == 0:
            k_left = jnp.where(lo_mask, k_all, 0.0)
            k_right = pltpu.roll(k_left, hd, 1)
        else:
            k_right = jnp.where(lo_mask, 0.0, k_all)
            k_left = pltpu.roll(k_right, hd, 1)
        v_h = v_t[par * hd:(par + 1) * hd]
        values.append((jnp.concatenate([v_h, zeros_half], axis=0).astype(BF16),
                       jnp.concatenate([zeros_half, v_h], axis=0).astype(BF16)))
        base = par * group * hd
        qs = jnp.concatenate([q_ref[:, base + pp * LANES: base + (pp + 1) * LANES] for pp in range(npair)], axis=0)
        qs_t = qs.T
        scores.append((_dot(k_left.astype(BF16), qs_t), _dot(k_right.astype(BF16), qs_t)))

    for par in range(2):
        base = par * group * hd
        o_pair_t = None
        for side in range(2):
            s = scores[par][side]
            head0 = (2 * hp + par) * group + side
            sink = jnp.full((1, nq), sink_ref[head0], F32)
            for pp in range(1, npair):
                sink = jnp.where(lane_q >= pp * w, sink_ref[head0 + 2 * pp], sink)
            sink = sink * LOG2E
            sl = s[:n_loc] + bias
            sc = s[n_loc:]
            m = jnp.maximum(jnp.maximum(jnp.max(sl, axis=0, keepdims=True), jnp.max(sc, axis=0, keepdims=True)),
                            sink)
            el = jnp.exp2(sl - m)
            ec = jnp.exp2(sc - m)
            den = jnp.sum(el, axis=0, keepdims=True) + jnp.sum(ec, axis=0, keepdims=True) + jnp.exp2(sink - m)
            e = jnp.concatenate([el, ec], axis=0).astype(BF16)
            o_t = _dot(values[par][side], e) / den
            o_pair_t = o_t if o_pair_t is None else o_pair_t + o_t
        o_pair = o_pair_t.T
        for pp in range(npair):
            o_ref[:, base + pp * LANES: base + (pp + 1) * LANES] = o_pair[pp * w:(pp + 1) * w].astype(o_ref.dtype)


def _attn_a(dm, z, sink, n_heads, n_kv):
    group = n_heads // n_kv
    qd, kvd = n_heads * A_HEAD_DIM, n_kv * A_HEAD_DIM
    w = WINDOW
    nb = dm.s // w
    ncb = dm.c // w
    lat_blocks = dm.t_lat // w
    qw = 2 * group * A_HEAD_DIM
    kcol, vcol = qd // LANES, (qd + kvd) // LANES

    def qrow(b, i):
        return jnp.where(i < nb, b * nb + i, lat_blocks + b * ncb + (i - nb))

    def krow(b, i, off):
        return b * nb + jnp.clip(jnp.minimum(i, nb - 1) + off, 0, nb - 1)

    def kv_spec(col, off):
        return pl.BlockSpec((w, LANES), lambda b, i, hp, s_: (krow(b, i, off), col + hp))

    def ctx_spec(col):
        return pl.BlockSpec((dm.c, LANES), lambda b, i, hp, s_: (dm.t_lat // dm.c + b, col + hp))

    grid_spec = pltpu.PrefetchScalarGridSpec(
        num_scalar_prefetch=1,
        grid=(dm.b, nb + ncb, n_kv // 2),
        in_specs=[
            pl.BlockSpec((w, qw), lambda b, i, hp, s_: (qrow(b, i), hp)),
            kv_spec(kcol, -1), kv_spec(kcol, 0), kv_spec(kcol, 1),
            kv_spec(vcol, -1), kv_spec(vcol, 0), kv_spec(vcol, 1),
            ctx_spec(kcol), ctx_spec(vcol),
        ],
        out_specs=pl.BlockSpec((w, qw), lambda b, i, hp, s_: (qrow(b, i), hp)),
    )
    return pl.pallas_call(
        functools.partial(_attn_a_kernel, nb=nb, group=group),
        grid_spec=grid_spec,
        out_shape=jax.ShapeDtypeStruct((dm.t_all, qd), BF16),
        compiler_params=_cparams("arbitrary", "arbitrary", "arbitrary"),
        name="attn_window",
    )(sink, z, z, z, z, z, z, z, z, z)


def _mm_res_kernel(a_ref, w_ref, *refs, n_lat_blocks):
    if n_lat_blocks is None:
        res_ref, gate_ref, o_ref = refs
        res = res_ref[...]
    else:
        rl_ref, rc_ref, gate_ref, o_ref = refs
        res = jnp.where(pl.program_id(0) < n_lat_blocks, rl_ref[...], rc_ref[...])
    o_ref[...] = res + gate_ref[0] * _dot(a_ref[...], w_ref[...])


def _mm_res(dm, a, w, res, mod3, gate_col, rows):
    tm = 2 * dm.tm if (rows % (2 * dm.tm) == 0 and dm.s % (2 * dm.tm) == 0) else dm.tm
    k, n = w.shape
    tn = min(512, n)
    split = isinstance(res, tuple)
    n_lat = dm.t_lat // tm
    if split:
        res_specs = [pl.BlockSpec((tm, tn), lambda i, j: (jnp.minimum(i, n_lat - 1), j)),
                     pl.BlockSpec((tm, tn), lambda i, j: (jnp.maximum(i - n_lat, 0), j))]
        res_args = res
    else:
        res_specs = [pl.BlockSpec((tm, tn), lambda i, j: (i, j))]
        res_args = (res,)
    return pl.pallas_call(
        functools.partial(_mm_res_kernel, n_lat_blocks=n_lat if split else None),
        grid=(rows // tm, n // tn),
        in_specs=[
            pl.BlockSpec((tm, k), lambda i, j: (i, 0)),
            pl.BlockSpec((k, tn), lambda i, j: (0, j)),
            *res_specs,
            dm.mod_spec(gate_col, tm, tn),
        ],
        out_specs=pl.BlockSpec((tm, tn), lambda i, j: (i, j)),
        out_shape=jax.ShapeDtypeStruct((rows, n), F32),
        compiler_params=_cparams("arbitrary", "arbitrary"),
        name="out_proj_residual",
    )(a, w, *res_args, mod3)


def _mod_router_kernel(x_ref, g_ref, sh_ref, sc_ref, wh_ref, wl_ref, br_ref, h_ref, route_ref, *, n_groups, n_experts):
    h = _modulate_val(x_ref[...], g_ref[...], sh_ref[0], sc_ref[0])
    h_ref[...] = h
    h_hi, h_lo = _split_bf16(h)
    wh = wh_ref[...]
    lg = _dot(h_hi, wh) + _dot(h_lo, wh) + _dot(h_hi, wl_ref[...]) + br_ref[...]

    per = n_experts // n_groups
    lane = lax.broadcasted_iota(jnp.int32, lg.shape, 1)
    lane_f = lane.astype(F32)
    big = float(LANES)

    def first_argmax(v, vmax):
        return jnp.min(jnp.where(v == vmax, lane_f, big), axis=1, keepdims=True)

    is_g = lane < n_groups
    g_l = jnp.where(is_g, lg, NEG_INF)
    gmax = jnp.max(g_l, axis=1, keepdims=True)
    grp = first_argmax(g_l, gmax)
    p_grp = 1.0 / jnp.sum(jnp.where(is_g, jnp.exp(g_l - gmax), 0.0), axis=1, keepdims=True)

    lo = n_groups + grp * per
    e_l = jnp.where((lane_f >= lo) & (lane_f < lo + per), lg, NEG_INF)
    t1 = jnp.max(e_l, axis=1, keepdims=True)
    i1 = first_argmax(e_l, t1)
    e_l2 = jnp.where(lane_f == i1, NEG_INF, e_l)
    t2 = jnp.max(e_l2, axis=1, keepdims=True)
    i2 = first_argmax(e_l2, t2)
    ratio = jnp.exp(t2 - t1)
    gate1 = p_grp / (1.0 + ratio)
    gate2 = gate1 * ratio
    route = jnp.where(lane == 0, i1 - n_groups,
                      jnp.where(lane == 1, i2 - n_groups,
                                jnp.where(lane == 2, gate1, jnp.where(lane == 3, gate2, 0.0))))
    route_ref[...] = route


def _mod_router(dm, x, g, mod3, shift_col, scale_col, w_r, b_r, rows, n_groups, n_experts):
    tm = min(dm.tm, 256)
    w_hi, w_lo = _split_bf16(w_r)
    kern = functools.partial(_mod_router_kernel, n_groups=n_groups, n_experts=n_experts)
    return pl.pallas_call(
        kern,
        grid=(rows // tm,),
        in_specs=[
            pl.BlockSpec((tm, dm.d), lambda i: (i, 0)),
            pl.BlockSpec((1, dm.d), lambda i: (0, 0)),
            dm.mod_spec(shift_col, tm),
            dm.mod_spec(scale_col, tm),
            pl.BlockSpec((dm.d, LANES), lambda i: (0, 0)),
            pl.BlockSpec((dm.d, LANES), lambda i: (0, 0)),
            pl.BlockSpec((1, LANES), lambda i: (0, 0)),
        ],
        out_specs=[
            pl.BlockSpec((tm, dm.d), lambda i: (i, 0)),
            pl.BlockSpec((tm, LANES), lambda i: (i, 0)),
        ],
        out_shape=[
            jax.ShapeDtypeStruct((rows, dm.d), F32),
            jax.ShapeDtypeStruct((rows, LANES), F32),
        ],
        compiler_params=_cparams("arbitrary"),
        name="modulate_router",
    )(x, g.reshape(1, dm.d), mod3, mod3, w_hi, w_lo, b_r)


def _dispatch(route, n_experts):
    eid = route[:, :2].astype(jnp.int32).reshape(-1)
    n_assign = eid.shape[0]
    order = jnp.argsort(eid).astype(jnp.int32)
    rank = jnp.argsort(order).astype(jnp.int32)
    onehot = eid[:, None] == jnp.arange(n_experts, dtype=jnp.int32)[None, :]
    counts = jnp.sum(onehot, axis=0, dtype=jnp.int32)
    padded = (counts + MOE_BLOCK - 1) // MOE_BLOCK * MOE_BLOCK
    p_end = jnp.cumsum(padded)
    p_start = p_end - padded
    c_start = jnp.cumsum(counts) - counts
    dest = rank + jnp.sum(jnp.where(onehot, (p_start - c_start)[None, :], 0), axis=1, dtype=jnp.int32)
    n_rows = (n_assign + n_experts * (MOE_BLOCK - 1) + MOE_BLOCK - 1) // MOE_BLOCK * MOE_BLOCK
    n_blocks = n_rows // MOE_BLOCK
    blk_start = jnp.arange(n_blocks, dtype=jnp.int32) * MOE_BLOCK
    blk_e = jnp.minimum(jnp.sum(p_end[None, :] <= blk_start[:, None], axis=1, dtype=jnp.int32), n_experts - 1)
    off = (blk_start - p_start[blk_e])[:, None] + jnp.arange(MOE_BLOCK, dtype=jnp.int32)[None, :]
    pos = jnp.clip(c_start[blk_e][:, None] + off, 0, n_assign - 1)
    src_tok = jnp.where(off < counts[blk_e][:, None], order[pos] // 2, 0).reshape(n_rows).astype(jnp.int32)
    n_used = (p_end[-1:] // MOE_BLOCK).astype(jnp.int32)
    ids = jnp.arange(n_experts, dtype=jnp.int32)
    later = (ids[None, :] > ids[:, None]) & (counts[None, :] > 0)
    nxt = jnp.min(jnp.where(later, ids[None, :], n_experts), axis=1)
    nxt_e = jnp.where(nxt < n_experts, nxt, -1).astype(jnp.int32)[blk_e]
    return blk_e, nxt_e, src_tok, n_used, dest, n_rows


def _moe_kernel(blk_e_ref, nxt_e_ref, src_ref, nused_ref, h_hbm, wgu_hbm, wdn_hbm, y_ref,
                xb, sem, wst_gu, wst_dn, wb_gu, wb_dn, wsem, *, d_exp, layer):
    b = pl.program_id(0)
    n_used = nused_ref[0]
    slot = b % 2
    e_cur = blk_e_ref[b]
    run_start = jnp.logical_or(b == 0, e_cur != blk_e_ref[jnp.maximum(b - 1, 0)])

    def row_copies(blk, slot_):
        return [pltpu.make_async_copy(h_hbm.at[pl.ds(src_ref[blk * MOE_BLOCK + r], 1), :],
                                      xb.at[slot_, pl.ds(r, 1), :], sem.at[slot_])
                for r in range(MOE_BLOCK)]

    def weight_copies(e):
        return (pltpu.make_async_copy(wgu_hbm.at[layer, e], wst_gu, wsem.at[0]),
                pltpu.make_async_copy(wdn_hbm.at[layer, e], wst_dn, wsem.at[1]))

    @pl.when(b == 0)
    def _():
        for cp in row_copies(0, 0):
            cp.start()
        for cp in weight_copies(e_cur):
            cp.start()

    @pl.when(b >= n_used)
    def _():
        y_ref[...] = jnp.zeros(y_ref.shape, y_ref.dtype)

    @pl.when(b < n_used)
    def _():
        @pl.when(run_start)
        def _():
            for cp in weight_copies(e_cur):
                cp.wait()
            rows_gu = wst_gu.shape[0] // WEIGHT_CAST_CHUNKS
            rows_dn = wst_dn.shape[0] // WEIGHT_CAST_CHUNKS

            def cast_chunk(ci, carry):
                r0 = pl.multiple_of(ci * rows_gu, rows_gu)
                wb_gu[pl.ds(r0, rows_gu), :] = wst_gu[pl.ds(r0, rows_gu), :].astype(BF16)
                r1 = pl.multiple_of(ci * rows_dn, rows_dn)
                wb_dn[pl.ds(r1, rows_dn), :] = wst_dn[pl.ds(r1, rows_dn), :].astype(BF16)
                return carry

            lax.fori_loop(0, WEIGHT_CAST_CHUNKS, cast_chunk, 0)
            nxt = nxt_e_ref[b]

            @pl.when(nxt >= 0)
            def _():
                for cp in weight_copies(nxt):
                    cp.start(priority=BULK_DMA_PRIORITY)

        for cp in row_copies(b, slot):
            cp.wait()

        @pl.when(b + 1 < n_used)
        def _():
            for cp in row_copies(b + 1, 1 - slot):
                cp.start()

        xrow = xb[slot].astype(BF16)
        gu = _dot(xrow, wb_gu[...])
        g, u = gu[:, :d_exp], gu[:, d_exp:]
        act = (g * jax.nn.sigmoid(g) * u).astype(BF16)
        y_ref[...] = _dot(act, wb_dn[...])


def _moe_experts(h, blk_e, nxt_e, src_tok, n_used, w_gu, w_dn, layer, n_rows):
    _, _, d, f2 = w_gu.shape
    d_exp = f2 // 2
    n_blocks = n_rows // MOE_BLOCK
    grid_spec = pltpu.PrefetchScalarGridSpec(
        num_scalar_prefetch=4,
        grid=(n_blocks,),
        in_specs=[
            pl.BlockSpec(memory_space=pl.ANY),
            pl.BlockSpec(memory_space=pl.ANY),
            pl.BlockSpec(memory_space=pl.ANY),
        ],
        out_specs=pl.BlockSpec((MOE_BLOCK, d), lambda b, be, ne, st, nu: (b, 0)),
        scratch_shapes=[
            pltpu.VMEM((2, MOE_BLOCK, d), F32),
            pltpu.SemaphoreType.DMA((2,)),
            pltpu.VMEM((d, f2), F32),
            pltpu.VMEM((d_exp, d), F32),
            pltpu.VMEM((d, f2), BF16),
            pltpu.VMEM((d_exp, d), BF16),
            pltpu.SemaphoreType.DMA((2,)),
        ],
    )
    return pl.pallas_call(
        functools.partial(_moe_kernel, d_exp=d_exp, layer=layer),
        grid_spec=grid_spec,
        out_shape=jax.ShapeDtypeStruct((n_rows, d), F32),
        compiler_params=_cparams("arbitrary"),
        name="moe_experts",
    )(blk_e, nxt_e, src_tok, n_used, h, w_gu, w_dn)


def _combine_kernel(dest_ref, y_hbm, x_ref, route_ref, g2_ref, *refs, tc, n_blocks, final):
    if final:
        fg_ref, out_ref, yb, sem = refs
    else:
        ng_ref, sh_ref, sc_ref, xo_ref, h_ref, yb, sem = refs
    t = pl.program_id(0)
    slot = t % 2

    def row_copies(blk, slot_):
        return [pltpu.make_async_copy(y_hbm.at[pl.ds(dest_ref[(blk * tc + r) * 2 + k], 1), :],
                                      yb.at[slot_, k, pl.ds(r, 1), :], sem.at[slot_])
                for r in range(tc) for k in range(2)]

    @pl.when(t == 0)
    def _():
        for cp in row_copies(0, 0):
            cp.start()

    for cp in row_copies(t, slot):
        cp.wait()

    @pl.when(t + 1 < n_blocks)
    def _():
        for cp in row_copies(t + 1, 1 - slot):
            cp.start()

    route = route_ref[...]
    y = route[:, 2:3] * yb[slot, 0] + route[:, 3:4] * yb[slot, 1]
    xn = x_ref[...] + g2_ref[0] * y
    if final:
        out_ref[...] = _rms(xn) * fg_ref[...]
    else:
        xo_ref[...] = xn
        h_ref[...] = _modulate_val(xn, ng_ref[...], sh_ref[0], sc_ref[0]).astype(h_ref.dtype)


def _combine(dm, y_buf, dest, x, route, mod3, gate_col, rows, *, final_g=None, next_g=None, next_mod3=None):
    tc = MOE_BLOCK
    n_blocks = rows // tc
    final = final_g is not None
    row_spec = pl.BlockSpec((tc, dm.d), lambda t, d_: (t, 0))
    vec_spec = pl.BlockSpec((1, dm.d), lambda t, d_: (0, 0))
    in_specs = [
        pl.BlockSpec(memory_space=pl.ANY),
        row_spec,
        pl.BlockSpec((tc, LANES), lambda t, d_: (t, 0)),
        dm.mod_spec(gate_col, tc),
    ]
    if final:
        in_specs += [vec_spec]
        args = (final_g.reshape(1, dm.d),)
        out_specs = row_spec
        out_shape = jax.ShapeDtypeStruct((rows, dm.d), F32)
    else:
        in_specs += [vec_spec, dm.mod_spec(0, tc), dm.mod_spec(1, tc)]
        args = (next_g.reshape(1, dm.d), next_mod3, next_mod3)
        out_specs = [row_spec, row_spec]
        out_shape = [jax.ShapeDtypeStruct((rows, dm.d), F32), jax.ShapeDtypeStruct((rows, dm.d), BF16)]
    grid_spec = pltpu.PrefetchScalarGridSpec(
        num_scalar_prefetch=1,
        grid=(n_blocks,),
        in_specs=in_specs,
        out_specs=out_specs,
        scratch_shapes=[
            pltpu.VMEM((2, 2, tc, dm.d), F32),
            pltpu.SemaphoreType.DMA((2,)),
        ],
    )
    return pl.pallas_call(
        functools.partial(_combine_kernel, tc=tc, n_blocks=n_blocks, final=final),
        grid_spec=grid_spec,
        out_shape=out_shape,
        compiler_params=_cparams("arbitrary"),
        name="moe_combine",
    )(dest, y_buf, x, route, mod3, *args)


def _dkv_kernel(a_ref, wq_ref, wkv_ref, gq_ref, gkv_ref, cos_ref, sin_ref, zq_ref, ckv_ref, kr_ref, *, kv_rank):
    a = a_ref[...]
    zq_ref[...] = (_rms(_dot(a, wq_ref[...])) * gq_ref[...]).astype(zq_ref.dtype)
    zkv = _dot(a, wkv_ref[...])
    ckv_ref[...] = (_rms(zkv[:, :kv_rank]) * gkv_ref[...]).astype(ckv_ref.dtype)
    kr_ref[...] = _rope128(zkv[:, kv_rank:], cos_ref[...], sin_ref[...]).astype(kr_ref.dtype)


def _dkv_proj(dm, h, wq, wkv, gq, gkv, cos_t, sin_t):
    tm = dm.tm
    q_rank = wq.shape[1]
    kv_rank = wkv.shape[1] - LANES
    rope_spec = pl.BlockSpec((tm, LANES), lambda i: (dm.ropeblk(i, tm), 0))
    return pl.pallas_call(
        functools.partial(_dkv_kernel, kv_rank=kv_rank),
        grid=(dm.t_all // tm,),
        in_specs=[
            pl.BlockSpec((tm, dm.d), lambda i: (i, 0)),
            pl.BlockSpec((dm.d, q_rank), lambda i: (0, 0)),
            pl.BlockSpec((dm.d, kv_rank + LANES), lambda i: (0, 0)),
            pl.BlockSpec((1, q_rank), lambda i: (0, 0)),
            pl.BlockSpec((1, kv_rank), lambda i: (0, 0)),
            rope_spec, rope_spec,
        ],
        out_specs=[
            pl.BlockSpec((tm, q_rank), lambda i: (i, 0)),
            pl.BlockSpec((tm, kv_rank), lambda i: (i, 0)),
            pl.BlockSpec((tm, LANES), lambda i: (i, 0)),
        ],
        out_shape=[
            jax.ShapeDtypeStruct((dm.t_all, q_rank), BF16),
            jax.ShapeDtypeStruct((dm.t_all, kv_rank), BF16),
            jax.ShapeDtypeStruct((dm.t_all, LANES), BF16),
        ],
        compiler_params=_cparams("arbitrary"),
        name="mla_down_proj",
    )(h, wq, wkv, gq.reshape(1, -1), gkv.reshape(1, -1), cos_t, sin_t)


def _qup_kernel(a_ref, wn_ref, wr_ref, cos_ref, sin_ref, o_ref, *, qscale, heads):
    a = a_ref[...]
    acc_n = _dot(a, wn_ref[...])
    acc_r = _dot(a, wr_ref[...])
    cos, sin = cos_ref[...], sin_ref[...]
    lo_mask = lax.broadcasted_iota(jnp.int32, (1, LANES), 1) < A_HEAD_DIM
    for h in range(heads):
        o_ref[:, h * 2 * LANES: h * 2 * LANES + LANES] = (acc_n[:, h * LANES:(h + 1) * LANES] * qscale).astype(
            o_ref.dtype)
    for g in range(heads // 2):
        r = _rope128(acc_r[:, g * LANES:(g + 1) * LANES], cos, sin) * qscale
        first = (2 * g) * 2 * LANES + LANES
        second = (2 * g + 1) * 2 * LANES + LANES
        o_ref[:, first: first + LANES] = jnp.where(lo_mask, r, 0.0).astype(o_ref.dtype)
        o_ref[:, second: second + LANES] = jnp.where(lo_mask, pltpu.roll(r, A_HEAD_DIM, 1), 0.0).astype(o_ref.dtype)


def _qup_proj(dm, zq, w_nope, w_rope, cos_t, sin_t, qscale, n_heads):
    tm = dm.tm
    k = zq.shape[1]
    hb = 4
    assert n_heads % hb == 0
    rope_spec = pl.BlockSpec((tm, LANES), lambda i, j: (dm.ropeblk(i, tm), 0))
    return pl.pallas_call(
        functools.partial(_qup_kernel, qscale=qscale, heads=hb),
        grid=(dm.t_lat // tm, n_heads // hb),
        in_specs=[
            pl.BlockSpec((tm, k), lambda i, j: (i, 0)),
            pl.BlockSpec((k, hb * LANES), lambda i, j: (0, j)),
            pl.BlockSpec((k, hb * A_HEAD_DIM), lambda i, j: (0, j)),
            rope_spec, rope_spec,
        ],
        out_specs=pl.BlockSpec((tm, hb * 2 * LANES), lambda i, j: (i, j)),
        out_shape=jax.ShapeDtypeStruct((dm.t_lat, n_heads * 2 * LANES), BF16),
        compiler_params=_cparams("arbitrary", "arbitrary"),
        name="mla_q_up",
    )(zq, w_nope, w_rope, cos_t, sin_t)


def _kvup_kernel(a_ref, wk_ref, wvt_ref, kr_ref, k_ref, vt_ref, *, heads):
    a = a_ref[...]
    kn = _dot(a, wk_ref[...])
    kr = kr_ref[...]
    for h in range(heads):
        k_ref[:, h * 2 * LANES: h * 2 * LANES + LANES] = kn[:, h * LANES: (h + 1) * LANES].astype(k_ref.dtype)
        k_ref[:, h * 2 * LANES + LANES: (h + 1) * 2 * LANES] = kr
    vt_ref[...] = _dot_nt(wvt_ref[...], a).astype(vt_ref.dtype)


def _kvup_proj(dm, ckv, w_uk, w_uvt, kr, n_heads):
    tm = dm.tm
    kv_rank = ckv.shape[1]
    hb = math.gcd(8, n_heads)
    return pl.pallas_call(
        functools.partial(_kvup_kernel, heads=hb),
        grid=(dm.t_all // tm, n_heads // hb),
        in_specs=[
            pl.BlockSpec((tm, kv_rank), lambda i, j: (i, 0)),
            pl.BlockSpec((kv_rank, hb * LANES), lambda i, j: (0, j)),
            pl.BlockSpec((hb * LANES, kv_rank), lambda i, j: (j, 0)),
            pl.BlockSpec((tm, LANES), lambda i, j: (i, 0)),
        ],
        out_specs=[
            pl.BlockSpec((tm, hb * 2 * LANES), lambda i, j: (i, j)),
            pl.BlockSpec((hb * LANES, tm), lambda i, j: (j, i)),
        ],
        out_shape=[
            jax.ShapeDtypeStruct((dm.t_all, n_heads * 2 * LANES), BF16),
            jax.ShapeDtypeStruct((n_heads * LANES, dm.t_all), BF16),
        ],
        compiler_params=_cparams("arbitrary", "arbitrary"),
        name="mla_kv_up",
    )(ckv, w_uk, w_uvt, kr)


def _mla_flash_kernel(q_ref, kl_ref, kc_ref, vl_ref, vc_ref, o_ref, m_sc, l_sc, acc_sc, sa_sc, sb_sc, pa_sc, pb_sc,
                      qt_sc, *, tk, n_chunks):
    tq = q_ref.shape[0]
    half = tq // 2
    m_sc[...] = jnp.full(m_sc.shape, NEG_INF, F32)
    l_sc[...] = jnp.zeros(l_sc.shape, F32)
    acc_sc[...] = jnp.zeros(acc_sc.shape, F32)
    pb_sc[...] = jnp.zeros(pb_sc.shape, pb_sc.dtype)
    qt_sc[...] = q_ref[...].T

    def k_chunk(ci):
        return kl_ref[pl.ds(pl.multiple_of(ci * tk, tk), tk), :]

    def vt_chunk(ci):
        return vl_ref[:, pl.ds(pl.multiple_of(ci * tk, tk), tk)]

    def softmax(s, sl):
        m_prev = m_sc[:, sl]
        m_new = jnp.maximum(m_prev, jnp.max(s, axis=0, keepdims=True))
        alpha = jnp.exp2(m_prev - m_new)
        p = jnp.exp2(s - m_new)
        l_sc[:, sl] = alpha * l_sc[:, sl] + jnp.sum(p, axis=0, keepdims=True)
        m_sc[:, sl] = m_new
        return alpha, p.astype(BF16)

    def stage(c_prev, c_next, s_cur, s_next, p_prev, p_cur):
        vt = vt_chunk(c_prev)
        kn = k_chunk(c_next)
        for hf in range(2):
            sl = slice(hf * half, (hf + 1) * half)
            pv = _dot(vt, p_prev[:, sl])
            s_next[:, sl] = _dot(kn, qt_sc[:, sl])
            alpha, p = softmax(s_cur[:, sl], sl)
            p_cur[:, sl] = p
            acc_sc[:, sl] = alpha * (acc_sc[:, sl] + pv)

    sa_sc[...] = _dot(k_chunk(0), qt_sc[...])

    def body(j, carry):
        c0 = 2 * j
        stage(jnp.maximum(c0 - 1, 0), c0 + 1, sa_sc, sb_sc, pb_sc, pa_sc)
        stage(c0, jnp.minimum(c0 + 2, n_chunks - 1), sb_sc, sa_sc, pa_sc, pb_sc)
        return carry

    lax.fori_loop(0, n_chunks // 2, body, 0)
    pv = _dot(vt_chunk(n_chunks - 1), pb_sc[...])
    alpha, p = softmax(_dot(kc_ref[...], qt_sc[...]), slice(None))
    acc = alpha * (acc_sc[...] + pv) + _dot(vc_ref[...], p)
    o_ref[...] = (acc / l_sc[...]).T.astype(o_ref.dtype)


def _mla_flash(dm, q, k, vt, n_heads):
    tq = min(512, dm.s)
    tk = min(512, dm.s // 2)
    nq = dm.s // tq
    n_chunks = dm.s // tk
    assert n_chunks % 2 == 0 and tq % (2 * LANES) == 0
    return pl.pallas_call(
        functools.partial(_mla_flash_kernel, tk=tk, n_chunks=n_chunks),
        grid=(dm.b, n_heads, nq),
        in_specs=[
            pl.BlockSpec((tq, 2 * LANES), lambda b, h, i: (b * nq + i, h)),
            pl.BlockSpec((dm.s, 2 * LANES), lambda b, h, i: (b, h)),
            pl.BlockSpec((dm.c, 2 * LANES), lambda b, h, i: (dm.t_lat // dm.c + b, h)),
            pl.BlockSpec((LANES, dm.s), lambda b, h, i: (h, b)),
            pl.BlockSpec((LANES, dm.c), lambda b, h, i: (h, dm.t_lat // dm.c + b)),
        ],
        out_specs=pl.BlockSpec((tq, LANES), lambda b, h, i: (b * nq + i, h)),
        out_shape=jax.ShapeDtypeStruct((dm.t_lat, n_heads * LANES), BF16),
        scratch_shapes=[
            pltpu.VMEM((1, tq), F32),
            pltpu.VMEM((1, tq), F32),
            pltpu.VMEM((LANES, tq), F32),
            pltpu.VMEM((tk, tq), F32),
            pltpu.VMEM((tk, tq), F32),
            pltpu.VMEM((tk, tq), BF16),
            pltpu.VMEM((tk, tq), BF16),
            pltpu.VMEM((2 * LANES, tq), BF16),
        ],
        compiler_params=_cparams("arbitrary", "arbitrary", "arbitrary"),
        name="mla_flash",
    )(q, k, k, vt, vt)


def _mixer_a(dm, h, w_qkv, sink, cos_t, sin_t):
    n_heads = sink.shape[0]
    n_kv = (w_qkv.shape[1] // A_HEAD_DIM - n_heads) // 2
    assert (n_heads // n_kv) % 2 == 0 and n_kv % 2 == 0
    z = _qkv_proj(dm, h, w_qkv.astype(BF16), cos_t, sin_t, n_heads * A_HEAD_DIM, n_kv * A_HEAD_DIM)
    return _attn_a(dm, z, sink, n_heads, n_kv)


def _mixer_b(dm, h, w_dkv, g_q, g_kv, w_uq, w_ukv, wo_rows, cos_t, sin_t):
    q_rank, kv_rank = g_q.shape[0], g_kv.shape[0]
    rope = w_dkv.shape[1] - q_rank - kv_rank
    n_heads = (w_uq.shape[1] - w_ukv.shape[1] + wo_rows) // rope
    nope = (w_ukv.shape[1] - wo_rows) // n_heads
    d_v = wo_rows // n_heads
    assert rope == A_HEAD_DIM and nope == LANES and d_v == LANES
    wq = w_dkv[:, :q_rank].astype(BF16)
    wkv = jnp.pad(w_dkv[:, q_rank:], ((0, 0), (0, LANES - rope))).astype(BF16)
    zq, ckv, kr = _dkv_proj(dm, h, wq, wkv, g_q, g_kv, cos_t, sin_t)
    w_uq3 = w_uq.reshape(q_rank, n_heads, nope + rope)
    w_uq_n = w_uq3[..., :nope].reshape(q_rank, n_heads * nope).astype(BF16)
    w_uq_r = w_uq3[..., nope:].reshape(q_rank, n_heads * rope).astype(BF16)
    qscale = (nope + rope) ** -0.5 * LOG2E
    q = _qup_proj(dm, zq, w_uq_n, w_uq_r, cos_t, sin_t, qscale, n_heads)
    w_ukv3 = w_ukv.reshape(kv_rank, n_heads, nope + d_v)
    w_uk = w_ukv3[..., :nope].reshape(kv_rank, n_heads * nope).astype(BF16)
    w_uvt = jnp.transpose(w_ukv3[..., nope:], (1, 2, 0)).reshape(n_heads * d_v, kv_rank).astype(BF16)
    k, vt = _kvup_proj(dm, ckv, w_uk, w_uvt, kr, n_heads)
    return _mla_flash(dm, q, k, vt, n_heads)


def kernel(x, c, ctx, c_ctx, ada_w, ada_b, norm_g, final_g, a_wqkv, a_wo, a_sink, b_wdkv, b_gq, b_gkv, b_wuq,
           b_wukv, b_wo, r_wg, r_bg, r_we, r_be, e_wgu, e_wdn):
    b, s, d = x.shape
    c_len = ctx.shape[1]
    depth = ada_w.shape[0]
    dm = _Dims(b, s, c_len, d)
    assert b + 1 <= MOD_ROWS and dm.t_lat % c_len == 0 and c_len % WINDOW == 0 and s % GRID_W == 0
    n_groups, n_experts = r_wg.shape[2], r_we.shape[2]
    assert n_groups + n_experts <= LANES

    cc = jnp.concatenate([c, c_ctx[None, :], jnp.zeros((MOD_ROWS - b - 1, d), F32)], axis=0)
    mod = _ada(cc, ada_w, ada_b)
    mod3 = [mod[i].reshape(MOD_ROWS, 1, 6 * d) for i in range(depth)]
    cos_t, sin_t = _rope_tables(s, A_HEAD_DIM, dm.tm)

    xa = (x.reshape(dm.t_lat, d), ctx.reshape(dm.t_ctx, d))
    h = xa + (norm_g[0, 0], mod3[0])
    out = None
    for i in range(depth):
        last = i == depth - 1
        j = i // 2
        rows = dm.t_lat if last else dm.t_all
        if i % 2 == 0:
            o = _mixer_a(dm, h, a_wqkv[j], a_sink[j], cos_t, sin_t)
            w_o = a_wo[j]
        else:
            o = _mixer_b(dm, h, b_wdkv[j], b_gq[j], b_gkv[j], b_wuq[j], b_wukv[j], b_wo.shape[1], cos_t, sin_t)
            w_o = b_wo[j]
        x1 = _mm_res(dm, o, w_o.astype(BF16), xa, mod3[i], 2, rows)
        w_r = jnp.pad(jnp.concatenate([r_wg[i], r_we[i]], axis=1), ((0, 0), (0, LANES - n_groups - n_experts)))
        b_r = jnp.pad(jnp.concatenate([r_bg[i], r_be[i]]), (0, LANES - n_groups - n_experts)).reshape(1, LANES)
        h2, route = _mod_router(dm, x1, norm_g[i, 1], mod3[i], 3, 4, w_r, b_r, rows, n_groups, n_experts)
        blk_e, nxt_e, src_tok, n_used, dest, n_rows = _dispatch(route, n_experts)
        y_buf = _moe_experts(h2, blk_e, nxt_e, src_tok, n_used, e_wgu, e_wdn, i, n_rows)
        if last:
            out = _combine(dm, y_buf, dest, x1, route, mod3[i], 5, rows, final_g=final_g)
        else:
            xa, h = _combine(dm, y_buf, dest, x1, route, mod3[i], 5, rows,
                             next_g=norm_g[i + 1, 0], next_mod3=mod3[i + 1])
    return out.reshape(b, s, d)
```

```python
import functools
import math

import jax
import jax.numpy as jnp
from jax import lax
from jax.experimental import pallas as pl
from jax.experimental.pallas import tpu as pltpu

F32 = jnp.float32
BF16 = jnp.bfloat16

EPS = 1e-6
NEG_INF = -1e30
ROPE_BASE = 10000.0
GRID_W = 64
A_HEAD_DIM = 64
WINDOW = 128
MOE_BLOCK = 128
LANES = 128
MOD_ROWS = 8
LOG2E = math.log2(math.e)
WEIGHT_CAST_CHUNKS = 8
BULK_DMA_PRIORITY = 1
MOD_CHUNK_ROWS = 32
VMEM_LIMIT_BYTES = 56 * 1024 * 1024


def _cparams(*sem):
    return pltpu.CompilerParams(dimension_semantics=sem, vmem_limit_bytes=VMEM_LIMIT_BYTES)


def _dot(a, b):
    return jnp.dot(a, b, preferred_element_type=F32)


def _dot_nt(a, b):
    return lax.dot_general(a, b, (((1,), (1,)), ((), ())), preferred_element_type=F32)


def _split_bf16(v):
    hi = v.astype(BF16)
    lo = (v - hi.astype(F32)).astype(BF16)
    return hi, lo


def _rms(x):
    return x * lax.rsqrt(jnp.mean(x * x, axis=-1, keepdims=True) + EPS)


def _modulate_val(x, g, shift, scale):
    return _rms(x) * g * (1.0 + scale) + shift


def _rope128(x, cos, sin_signed):
    lane = lax.broadcasted_iota(jnp.int32, x.shape, 1)
    first_half = (lane & 32) == 0
    partner = jnp.where(first_half, pltpu.roll(x, 96, 1), pltpu.roll(x, 32, 1))
    return x * cos + partner * sin_signed


def _ada_kernel(c_ref, w_ref, b_ref, o_ref):
    c = c_ref[...]
    s = c * jax.nn.sigmoid(c)
    s_hi, s_lo = _split_bf16(s)
    w_hi, w_lo = _split_bf16(w_ref[...])
    r = _dot(jnp.concatenate([s_hi, s_lo], axis=0), w_hi)
    o_ref[...] = r[:MOD_ROWS] + r[MOD_ROWS:] + _dot(s_hi, w_lo) + b_ref[...]


def _ada(cc, ada_w, ada_b):
    depth, d, n = ada_w.shape
    tn = 512
    return pl.pallas_call(
        _ada_kernel,
        grid=(depth, n // tn),
        in_specs=[
            pl.BlockSpec((MOD_ROWS, d), lambda l, j: (0, 0)),
            pl.BlockSpec((None, d, tn), lambda l, j: (l, 0, j)),
            pl.BlockSpec((None, 1, tn), lambda l, j: (l, 0, j)),
        ],
        out_specs=pl.BlockSpec((None, MOD_ROWS, tn), lambda l, j: (l, 0, j)),
        out_shape=jax.ShapeDtypeStruct((depth, MOD_ROWS, n), F32),
        compiler_params=_cparams("arbitrary", "arbitrary"),
        name="ada",
    )(cc, ada_w, ada_b.reshape(depth, 1, n))


class _Dims:
    def __init__(self, b, s, c, d):
        self.b, self.s, self.c, self.d = b, s, c, d
        self.t_lat, self.t_ctx = b * s, b * c
        self.t_all = self.t_lat + self.t_ctx
        self.tm = next(t for t in (512, 256, 128) if s % t == 0 and self.t_ctx % t == 0)

    def modrow(self, i, tm):
        return jnp.minimum(i // (self.s // tm), self.b)

    def ropeblk(self, i, tm):
        return jnp.where(i < self.t_lat // tm, i % (self.s // tm), self.s // tm)

    def mod_spec(self, col, tm, width=None):
        width = self.d if width is None else width
        per = self.d // width
        if width == self.d:
            return pl.BlockSpec((1, 1, width), lambda i, *_: (self.modrow(i, tm), 0, col))
        return pl.BlockSpec((1, 1, width), lambda i, j, *_: (self.modrow(i, tm), 0, col * per + j))


def _rope_tables(s, d_rot, pad_rows):
    rows = s // GRID_W
    row = jnp.repeat(jnp.arange(rows, dtype=F32), GRID_W)
    col = jnp.tile(jnp.arange(GRID_W, dtype=F32), rows)
    n_freq = d_rot // 4
    inv = ROPE_BASE ** (-jnp.arange(n_freq, dtype=F32) / n_freq)
    ang = jnp.concatenate([row[:, None] * inv, col[:, None] * inv], axis=-1)
    cos, sin = jnp.cos(ang), jnp.sin(ang)
    reps = LANES // d_rot
    cos_t = jnp.tile(cos, (1, 2 * reps))
    sin_t = jnp.tile(jnp.concatenate([-sin, sin], axis=-1), (1, reps))
    cos_t = jnp.concatenate([cos_t, jnp.ones((pad_rows, LANES), F32)], axis=0)
    sin_t = jnp.concatenate([sin_t, jnp.zeros((pad_rows, LANES), F32)], axis=0)
    return cos_t, sin_t


def _qkv_kernel(*refs, n_q_blocks, n_rope_blocks, qscale, n_lat_blocks):
    i = pl.program_id(0)
    j = pl.program_id(1)
    if n_lat_blocks is None:
        a_ref, w_ref, cos_ref, sin_ref, o_ref = refs
    else:
        xl_ref, xc_ref, g_ref, sh_ref, sc_ref, w_ref, cos_ref, sin_ref, o_ref, a_ref = refs

        @pl.when(j == 0)
        def _():
            def mod_rows(r, carry):
                rows = pl.ds(pl.multiple_of(r * MOD_CHUNK_ROWS, MOD_CHUNK_ROWS), MOD_CHUNK_ROWS)
                x = jnp.where(i < n_lat_blocks, xl_ref[rows, :], xc_ref[rows, :])
                a_ref[rows, :] = _modulate_val(x, g_ref[...], sh_ref[0], sc_ref[0]).astype(a_ref.dtype)
                return carry

            lax.fori_loop(0, a_ref.shape[0] // MOD_CHUNK_ROWS, mod_rows, 0)
    acc = _dot(a_ref[...], w_ref[...])

    @pl.when(j >= n_rope_blocks)
    def _():
        o_ref[...] = acc.astype(o_ref.dtype)

    @pl.when(j < n_rope_blocks)
    def _():
        scale = jnp.where(j < n_q_blocks, qscale, 1.0).astype(F32)
        cos, sin = cos_ref[...], sin_ref[...]
        for g in range(acc.shape[1] // LANES):
            sl = slice(g * LANES, (g + 1) * LANES)
            o_ref[:, sl] = (_rope128(acc[:, sl], cos, sin) * scale).astype(o_ref.dtype)


def _qkv_proj(dm, h, w, cos_t, sin_t, qd, kvd):
    tm = dm.tm
    n = w.shape[1]
    tn = math.gcd(512, kvd)
    fused = isinstance(h, tuple)
    n_lat = dm.t_lat // tm
    kern = functools.partial(_qkv_kernel, n_q_blocks=qd // tn, n_rope_blocks=(qd + kvd) // tn,
                             qscale=A_HEAD_DIM ** -0.5 * LOG2E,
                             n_lat_blocks=n_lat if fused else None)
    tail_specs = [
        pl.BlockSpec((dm.d, tn), lambda i, j: (0, j)),
        pl.BlockSpec((tm, LANES), lambda i, j: (dm.ropeblk(i, tm), 0)),
        pl.BlockSpec((tm, LANES), lambda i, j: (dm.ropeblk(i, tm), 0)),
    ]
    if fused:
        x_lat, x_ctx, gain, mod3 = h
        in_specs = [
            pl.BlockSpec((tm, dm.d), lambda i, j: (jnp.minimum(i, n_lat - 1), 0)),
            pl.BlockSpec((tm, dm.d), lambda i, j: (jnp.maximum(i - n_lat, 0), 0)),
            pl.BlockSpec((1, dm.d), lambda i, j: (0, 0)),
            pl.BlockSpec((1, 1, dm.d), lambda i, j: (dm.modrow(i, tm), 0, 0)),
            pl.BlockSpec((1, 1, dm.d), lambda i, j: (dm.modrow(i, tm), 0, 1)),
        ] + tail_specs
        args = (x_lat, x_ctx, gain.reshape(1, dm.d), mod3, mod3, w, cos_t, sin_t)
        scratch = [pltpu.VMEM((tm, dm.d), BF16)]
    else:
        in_specs = [pl.BlockSpec((tm, dm.d), lambda i, j: (i, 0))] + tail_specs
        args = (h, w, cos_t, sin_t)
        scratch = []
    return pl.pallas_call(
        kern,
        grid=(dm.t_all // tm, n // tn),
        in_specs=in_specs,
        out_specs=pl.BlockSpec((tm, tn), lambda i, j: (i, j)),
        out_shape=jax.ShapeDtypeStruct((dm.t_all, n), BF16),
        scratch_shapes=scratch,
        compiler_params=_cparams("arbitrary", "arbitrary"),
        name="qkv_proj",
    )(*args)


def _attn_a_kernel(sink_ref, q_ref, kp_ref, kc_ref, kn_ref, vp_ref, vc_ref, vn_ref, kx_ref, vx_ref, o_ref,
                   *, nb, group):
    i = pl.program_id(1)
    hp = pl.program_id(2)
    npair = group // 2
    w = WINDOW
    hd = A_HEAD_DIM
    n_loc = 3 * w
    nq = npair * w
    lane = lax.broadcasted_iota(jnp.int32, (1, LANES), 1)
    lo_mask = lane < hd
    lane_q = lax.broadcasted_iota(jnp.int32, (1, nq), 1)

    k_all = jnp.concatenate([kp_ref[...], kc_ref[...], kn_ref[...], kx_ref[...]], axis=0).astype(F32)
    v_all = jnp.concatenate([vp_ref[...], vc_ref[...], vn_ref[...], vx_ref[...]], axis=0).astype(F32)
    v_t = v_all.T
    zeros_half = jnp.zeros((hd, v_t.shape[1]), F32)

    c = lax.broadcasted_iota(jnp.int32, (n_loc, w), 0)
    r = lax.broadcasted_iota(jnp.int32, (n_loc, w), 1)
    rel = c - w - r
    valid = ((rel >= -w) & (rel <= w) & ((c >= w) | (i > 0)) & ((c < 2 * w) | (i < nb - 1)) & (i < nb))
    bias = jnp.where(valid, 0.0, NEG_INF).astype(F32)
    bias = jnp.concatenate([bias] * npair, axis=1)

    scores, values = [], []
    for par in range(2):
        if par == 0:
            k_left = jnp.where(lo_mask, k_all, 0.0)
            k_right = pltpu.roll(k_left, hd, 1)
        else:
            k_right = jnp.where(lo_mask, 0.0, k_all)
            k_left = pltpu.roll(k_right, hd, 1)
        v_h = v_t[par * hd:(par + 1) * hd]
        values.append((jnp.concatenate([v_h, zeros_half], axis=0).astype(BF16),
                       jnp.concatenate([zeros_half, v_h], axis=0).astype(BF16)))
        base = par * group * hd
        qs = jnp.concatenate([q_ref[:, base + pp * LANES: base + (pp + 1) * LANES] for pp in range(npair)], axis=0)
        qs_t = qs.T
        scores.append((_dot(k_left.astype(BF16), qs_t), _dot(k_right.astype(BF16), qs_t)))

    for par in range(2):
        base = par * group * hd
        o_pair_t = None
        for side in range(2):
            s = scores[par][side]
            head0 = (2 * hp + par) * group + side
            sink = jnp.full((1, nq), sink_ref[head0], F32)
            for pp in range(1, npair):
                sink = jnp.where(lane_q >= pp * w, sink_ref[head0 + 2 * pp], sink)
            sink = sink * LOG2E
            sl = s[:n_loc] + bias
            sc = s[n_loc:]
            m = jnp.maximum(jnp.maximum(jnp.max(sl, axis=0, keepdims=True), jnp.max(sc, axis=0, keepdims=True)),
                            sink)
            el = jnp.exp2(sl - m)
            ec = jnp.exp2(sc - m)
            den = jnp.sum(el, axis=0, keepdims=True) + jnp.sum(ec, axis=0, keepdims=True) + jnp.exp2(sink - m)
            e = jnp.concatenate([el, ec], axis=0).astype(BF16)
            o_t = _dot(values[par][side], e) / den
            o_pair_t = o_t if o_pair_t is None else o_pair_t + o_t
        o_pair = o_pair_t.T
        for pp in range(npair):
            o_ref[:, base + pp * LANES: base + (pp + 1) * LANES] = o_pair[pp * w:(pp + 1) * w].astype(o_ref.dtype)


def _attn_a(dm, z, sink, n_heads, n_kv):
    group = n_heads // n_kv
    qd, kvd = n_heads * A_HEAD_DIM, n_kv * A_HEAD_DIM
    w = WINDOW
    nb = dm.s // w
    ncb = dm.c // w
    lat_blocks = dm.t_lat // w
    qw = 2 * group * A_HEAD_DIM
    kcol, vcol = qd // LANES, (qd + kvd) // LANES

    def qrow(b, i):
        return jnp.where(i < nb, b * nb + i, lat_blocks + b * ncb + (i - nb))

    def krow(b, i, off):
        return b * nb + jnp.clip(jnp.minimum(i, nb - 1) + off, 0, nb - 1)

    def kv_spec(col, off):
        return pl.BlockSpec((w, LANES), lambda b, i, hp, s_: (krow(b, i, off), col + hp))

    def ctx_spec(col):
        return pl.BlockSpec((dm.c, LANES), lambda b, i, hp, s_: (dm.t_lat // dm.c + b, col + hp))

    grid_spec = pltpu.PrefetchScalarGridSpec(
        num_scalar_prefetch=1,
        grid=(dm.b, nb + ncb, n_kv // 2),
        in_specs=[
            pl.BlockSpec((w, qw), lambda b, i, hp, s_: (qrow(b, i), hp)),
            kv_spec(kcol, -1), kv_spec(kcol, 0), kv_spec(kcol, 1),
            kv_spec(vcol, -1), kv_spec(vcol, 0), kv_spec(vcol, 1),
            ctx_spec(kcol), ctx_spec(vcol),
        ],
        out_specs=pl.BlockSpec((w, qw), lambda b, i, hp, s_: (qrow(b, i), hp)),
    )
    return pl.pallas_call(
        functools.partial(_attn_a_kernel, nb=nb, group=group),
        grid_spec=grid_spec,
        out_shape=jax.ShapeDtypeStruct((dm.t_all, qd), BF16),
        compiler_params=_cparams("arbitrary", "arbitrary", "arbitrary"),
        name="attn_window",
    )(sink, z, z, z, z, z, z, z, z, z)


def _mm_res_kernel(a_ref, w_ref, *refs, n_lat_blocks):
    if n_lat_blocks is None:
        res_ref, gate_ref, o_ref = refs
        res = res_ref[...]
    else:
        rl_ref, rc_ref, gate_ref, o_ref = refs
        res = jnp.where(pl.program_id(0) < n_lat_blocks, rl_ref[...], rc_ref[...])
    o_ref[...] = res + gate_ref[0] * _dot(a_ref[...], w_ref[...])


def _mm_res(dm, a, w, res, mod3, gate_col, rows):
    tm = 2 * dm.tm if (rows % (2 * dm.tm) == 0 and dm.s % (2 * dm.tm) == 0) else dm.tm
    k, n = w.shape
    tn = min(512, n)
    split = isinstance(res, tuple)
    n_lat = dm.t_lat // tm
    if split:
        res_specs = [pl.BlockSpec((tm, tn), lambda i, j: (jnp.minimum(i, n_lat - 1), j)),
                     pl.BlockSpec((tm, tn), lambda i, j: (jnp.maximum(i - n_lat, 0), j))]
        res_args = res
    else:
        res_specs = [pl.BlockSpec((tm, tn), lambda i, j: (i, j))]
        res_args = (res,)
    return pl.pallas_call(
        functools.partial(_mm_res_kernel, n_lat_blocks=n_lat if split else None),
        grid=(rows // tm, n // tn),
        in_specs=[
            pl.BlockSpec((tm, k), lambda i, j: (i, 0)),
            pl.BlockSpec((k, tn), lambda i, j: (0, j)),
            *res_specs,
            dm.mod_spec(gate_col, tm, tn),
        ],
        out_specs=pl.BlockSpec((tm, tn), lambda i, j: (i, j)),
        out_shape=jax.ShapeDtypeStruct((rows, n), F32),
        compiler_params=_cparams("arbitrary", "arbitrary"),
        name="out_proj_residual",
    )(a, w, *res_args, mod3)


def _mod_router_kernel(x_ref, g_ref, sh_ref, sc_ref, wh_ref, wl_ref, br_ref, h_ref, route_ref, *, n_groups, n_experts):
    h = _modulate_val(x_ref[...], g_ref[...], sh_ref[0], sc_ref[0])
    h_ref[...] = h
    h_hi, h_lo = _split_bf16(h)
    wh = wh_ref[...]
    lg = _dot(h_hi, wh) + _dot(h_lo, wh) + _dot(h_hi, wl_ref[...]) + br_ref[...]

    per = n_experts // n_groups
    lane = lax.broadcasted_iota(jnp.int32, lg.shape, 1)
    lane_f = lane.astype(F32)
    big = float(LANES)

    def first_argmax(v, vmax):
        return jnp.min(jnp.where(v == vmax, lane_f, big), axis=1, keepdims=True)

    is_g = lane < n_groups
    g_l = jnp.where(is_g, lg, NEG_INF)
    gmax = jnp.max(g_l, axis=1, keepdims=True)
    grp = first_argmax(g_l, gmax)
    p_grp = 1.0 / jnp.sum(jnp.where(is_g, jnp.exp(g_l - gmax), 0.0), axis=1, keepdims=True)

    lo = n_groups + grp * per
    e_l = jnp.where((lane_f >= lo) & (lane_f < lo + per), lg, NEG_INF)
    t1 = jnp.max(e_l, axis=1, keepdims=True)
    i1 = first_argmax(e_l, t1)
    e_l2 = jnp.where(lane_f == i1, NEG_INF, e_l)
    t2 = jnp.max(e_l2, axis=1, keepdims=True)
    i2 = first_argmax(e_l2, t2)
    ratio = jnp.exp(t2 - t1)
    gate1 = p_grp / (1.0 + ratio)
    gate2 = gate1 * ratio
    route = jnp.where(lane == 0, i1 - n_groups,
                      jnp.where(lane == 1, i2 - n_groups,
                                jnp.where(lane == 2, gate1, jnp.where(lane == 3, gate2, 0.0))))
    route_ref[...] = route


def _mod_router(dm, x, g, mod3, shift_col, scale_col, w_r, b_r, rows, n_groups, n_experts):
    tm = min(dm.tm, 256)
    w_hi, w_lo = _split_bf16(w_r)
    kern = functools.partial(_mod_router_kernel, n_groups=n_groups, n_experts=n_experts)
    return pl.pallas_call(
        kern,
        grid=(rows // tm,),
        in_specs=[
            pl.BlockSpec((tm, dm.d), lambda i: (i, 0)),
            pl.BlockSpec((1, dm.d), lambda i: (0, 0)),
            dm.mod_spec(shift_col, tm),
            dm.mod_spec(scale_col, tm),
            pl.BlockSpec((dm.d, LANES), lambda i: (0, 0)),
            pl.BlockSpec((dm.d, LANES), lambda i: (0, 0)),
            pl.BlockSpec((1, LANES), lambda i: (0, 0)),
        ],
        out_specs=[
            pl.BlockSpec((tm, dm.d), lambda i: (i, 0)),
            pl.BlockSpec((tm, LANES), lambda i: (i, 0)),
        ],
        out_shape=[
            jax.ShapeDtypeStruct((rows, dm.d), F32),
            jax.ShapeDtypeStruct((rows, LANES), F32),
        ],
        compiler_params=_cparams("arbitrary"),
        name="modulate_router",
    )(x, g.reshape(1, dm.d), mod3, mod3, w_hi, w_lo, b_r)


def _dispatch(route, n_experts):
    eid = route[:, :2].astype(jnp.int32).reshape(-1)
    n_assign = eid.shape[0]
    order = jnp.argsort(eid).astype(jnp.int32)
    rank = jnp.argsort(order).astype(jnp.int32)
    onehot = eid[:, None] == jnp.arange(n_experts, dtype=jnp.int32)[None, :]
    counts = jnp.sum(onehot, axis=0, dtype=jnp.int32)
    padded = (counts + MOE_BLOCK - 1) // MOE_BLOCK * MOE_BLOCK
    p_end = jnp.cumsum(padded)
    p_start = p_end - padded
    c_start = jnp.cumsum(counts) - counts
    dest = rank + jnp.sum(jnp.where(onehot, (p_start - c_start)[None, :], 0), axis=1, dtype=jnp.int32)
    n_rows = (n_assign + n_experts * (MOE_BLOCK - 1) + MOE_BLOCK - 1) // MOE_BLOCK * MOE_BLOCK
    n_blocks = n_rows // MOE_BLOCK
    blk_start = jnp.arange(n_blocks, dtype=jnp.int32) * MOE_BLOCK
    blk_e = jnp.minimum(jnp.sum(p_end[None, :] <= blk_start[:, None], axis=1, dtype=jnp.int32), n_experts - 1)
    off = (blk_start - p_start[blk_e])[:, None] + jnp.arange(MOE_BLOCK, dtype=jnp.int32)[None, :]
    pos = jnp.clip(c_start[blk_e][:, None] + off, 0, n_assign - 1)
    src_tok = jnp.where(off < counts[blk_e][:, None], order[pos] // 2, 0).reshape(n_rows).astype(jnp.int32)
    n_used = (p_end[-1:] // MOE_BLOCK).astype(jnp.int32)
    ids = jnp.arange(n_experts, dtype=jnp.int32)
    later = (ids[None, :] > ids[:, None]) & (counts[None, :] > 0)
    nxt = jnp.min(jnp.where(later, ids[None, :], n_experts), axis=1)
    nxt_e = jnp.where(nxt < n_experts, nxt, -1).astype(jnp.int32)[blk_e]
    return blk_e, nxt_e, src_tok, n_used, dest, n_rows


def _moe_kernel(blk_e_ref, nxt_e_ref, src_ref, nused_ref, h_hbm, wgu_hbm, wdn_hbm, y_ref,
                xb, sem, wst_gu, wst_dn, wb_gu, wb_dn, wsem, *, d_exp, layer):
    b = pl.program_id(0)
    n_used = nused_ref[0]
    slot = b % 2
    e_cur = blk_e_ref[b]
    run_start = jnp.logical_or(b == 0, e_cur != blk_e_ref[jnp.maximum(b - 1, 0)])

    def row_copies(blk, slot_):
        return [pltpu.make_async_copy(h_hbm.at[pl.ds(src_ref[blk * MOE_BLOCK + r], 1), :],
                                      xb.at[slot_, pl.ds(r, 1), :], sem.at[slot_])
                for r in range(MOE_BLOCK)]

    def weight_copies(e):
        return (pltpu.make_async_copy(wgu_hbm.at[layer, e], wst_gu, wsem.at[0]),
                pltpu.make_async_copy(wdn_hbm.at[layer, e], wst_dn, wsem.at[1]))

    @pl.when(b == 0)
    def _():
        for cp in row_copies(0, 0):
            cp.start()
        for cp in weight_copies(e_cur):
            cp.start()

    @pl.when(b >= n_used)
    def _():
        y_ref[...] = jnp.zeros(y_ref.shape, y_ref.dtype)

    @pl.when(b < n_used)
    def _():
        @pl.when(run_start)
        def _():
            for cp in weight_copies(e_cur):
                cp.wait()
            rows_gu = wst_gu.shape[0] // WEIGHT_CAST_CHUNKS
            rows_dn = wst_dn.shape[0] // WEIGHT_CAST_CHUNKS

            def cast_chunk(ci, carry):
                r0 = pl.multiple_of(ci * rows_gu, rows_gu)
                wb_gu[pl.ds(r0, rows_gu), :] = wst_gu[pl.ds(r0, rows_gu), :].astype(BF16)
                r1 = pl.multiple_of(ci * rows_dn, rows_dn)
                wb_dn[pl.ds(r1, rows_dn), :] = wst_dn[pl.ds(r1, rows_dn), :].astype(BF16)
                return carry

            lax.fori_loop(0, WEIGHT_CAST_CHUNKS, cast_chunk, 0)
            nxt = nxt_e_ref[b]

            @pl.when(nxt >= 0)
            def _():
                for cp in weight_copies(nxt):
                    cp.start(priority=BULK_DMA_PRIORITY)

        for cp in row_copies(b, slot):
            cp.wait()

        @pl.when(b + 1 < n_used)
        def _():
            for cp in row_copies(b + 1, 1 - slot):
                cp.start()

        xrow = xb[slot].astype(BF16)
        gu = _dot(xrow, wb_gu[...])
        g, u = gu[:, :d_exp], gu[:, d_exp:]
        act = (g * jax.nn.sigmoid(g) * u).astype(BF16)
        y_ref[...] = _dot(act, wb_dn[...])


def _moe_experts(h, blk_e, nxt_e, src_tok, n_used, w_gu, w_dn, layer, n_rows):
    _, _, d, f2 = w_gu.shape
    d_exp = f2 // 2
    n_blocks = n_rows // MOE_BLOCK
    grid_spec = pltpu.PrefetchScalarGridSpec(
        num_scalar_prefetch=4,
        grid=(n_blocks,),
        in_specs=[
            pl.BlockSpec(memory_space=pl.ANY),
            pl.BlockSpec(memory_space=pl.ANY),
            pl.BlockSpec(memory_space=pl.ANY),
        ],
        out_specs=pl.BlockSpec((MOE_BLOCK, d), lambda b, be, ne, st, nu: (b, 0)),
        scratch_shapes=[
            pltpu.VMEM((2, MOE_BLOCK, d), F32),
            pltpu.SemaphoreType.DMA((2,)),
            pltpu.VMEM((d, f2), F32),
            pltpu.VMEM((d_exp, d), F32),
            pltpu.VMEM((d, f2), BF16),
            pltpu.VMEM((d_exp, d), BF16),
            pltpu.SemaphoreType.DMA((2,)),
        ],
    )
    return pl.pallas_call(
        functools.partial(_moe_kernel, d_exp=d_exp, layer=layer),
        grid_spec=grid_spec,
        out_shape=jax.ShapeDtypeStruct((n_rows, d), F32),
        compiler_params=_cparams("arbitrary"),
        name="moe_experts",
    )(blk_e, nxt_e, src_tok, n_used, h, w_gu, w_dn)


def _combine_kernel(dest_ref, y_hbm, x_ref, route_ref, g2_ref, *refs, tc, n_blocks, final):
    if final:
        fg_ref, out_ref, yb, sem = refs
    else:
        ng_ref, sh_ref, sc_ref, xo_ref, h_ref, yb, sem = refs
    t = pl.program_id(0)
    slot = t % 2

    def row_copies(blk, slot_):
        return [pltpu.make_async_copy(y_hbm.at[pl.ds(dest_ref[(blk * tc + r) * 2 + k], 1), :],
                                      yb.at[slot_, k, pl.ds(r, 1), :], sem.at[slot_])
                for r in range(tc) for k in range(2)]

    @pl.when(t == 0)
    def _():
        for cp in row_copies(0, 0):
            cp.start()

    for cp in row_copies(t, slot):
        cp.wait()

    @pl.when(t + 1 < n_blocks)
    def _():
        for cp in row_copies(t + 1, 1 - slot):
            cp.start()

    route = route_ref[...]
    y = route[:, 2:3] * yb[slot, 0] + route[:, 3:4] * yb[slot, 1]
    xn = x_ref[...] + g2_ref[0] * y
    if final:
        out_ref[...] = _rms(xn) * fg_ref[...]
    else:
        xo_ref[...] = xn
        h_ref[...] = _modulate_val(xn, ng_ref[...], sh_ref[0], sc_ref[0]).astype(h_ref.dtype)


def _combine(dm, y_buf, dest, x, route, mod3, gate_col, rows, *, final_g=None, next_g=None, next_mod3=None):
    tc = MOE_BLOCK
    n_blocks = rows // tc
    final = final_g is not None
    row_spec = pl.BlockSpec((tc, dm.d), lambda t, d_: (t, 0))
    vec_spec = pl.BlockSpec((1, dm.d), lambda t, d_: (0, 0))
    in_specs = [
        pl.BlockSpec(memory_space=pl.ANY),
        row_spec,
        pl.BlockSpec((tc, LANES), lambda t, d_: (t, 0)),
        dm.mod_spec(gate_col, tc),
    ]
    if final:
        in_specs += [vec_spec]
        args = (final_g.reshape(1, dm.d),)
        out_specs = row_spec
        out_shape = jax.ShapeDtypeStruct((rows, dm.d), F32)
    else:
        in_specs += [vec_spec, dm.mod_spec(0, tc), dm.mod_spec(1, tc)]
        args = (next_g.reshape(1, dm.d), next_mod3, next_mod3)
        out_specs = [row_spec, row_spec]
        out_shape = [jax.ShapeDtypeStruct((rows, dm.d), F32), jax.ShapeDtypeStruct((rows, dm.d), BF16)]
    grid_spec = pltpu.PrefetchScalarGridSpec(
        num_scalar_prefetch=1,
        grid=(n_blocks,),
        in_specs=in_specs,
        out_specs=out_specs,
        scratch_shapes=[
            pltpu.VMEM((2, 2, tc, dm.d), F32),
            pltpu.SemaphoreType.DMA((2,)),
        ],
    )
    return pl.pallas_call(
        functools.partial(_combine_kernel, tc=tc, n_blocks=n_blocks, final=final),
        grid_spec=grid_spec,
        out_shape=out_shape,
        compiler_params=_cparams("arbitrary"),
        name="moe_combine",
    )(dest, y_buf, x, route, mod3, *args)


def _dkv_kernel(a_ref, wq_ref, wkv_ref, gq_ref, gkv_ref, cos_ref, sin_ref, zq_ref, ckv_ref, kr_ref, *, kv_rank):
    a = a_ref[...]
    zq_ref[...] = (_rms(_dot(a, wq_ref[...])) * gq_ref[...]).astype(zq_ref.dtype)
    zkv = _dot(a, wkv_ref[...])
    ckv_ref[...] = (_rms(zkv[:, :kv_rank]) * gkv_ref[...]).astype(ckv_ref.dtype)
    kr_ref[...] = _rope128(zkv[:, kv_rank:], cos_ref[...], sin_ref[...]).astype(kr_ref.dtype)


def _dkv_proj(dm, h, wq, wkv, gq, gkv, cos_t, sin_t):
    tm = dm.tm
    q_rank = wq.shape[1]
    kv_rank = wkv.shape[1] - LANES
    rope_spec = pl.BlockSpec((tm, LANES), lambda i: (dm.ropeblk(i, tm), 0))
    return pl.pallas_call(
        functools.partial(_dkv_kernel, kv_rank=kv_rank),
        grid=(dm.t_all // tm,),
        in_specs=[
            pl.BlockSpec((tm, dm.d), lambda i: (i, 0)),
            pl.BlockSpec((dm.d, q_rank), lambda i: (0, 0)),
            pl.BlockSpec((dm.d, kv_rank + LANES), lambda i: (0, 0)),
            pl.BlockSpec((1, q_rank), lambda i: (0, 0)),
            pl.BlockSpec((1, kv_rank), lambda i: (0, 0)),
            rope_spec, rope_spec,
        ],
        out_specs=[
            pl.BlockSpec((tm, q_rank), lambda i: (i, 0)),
            pl.BlockSpec((tm, kv_rank), lambda i: (i, 0)),
            pl.BlockSpec((tm, LANES), lambda i: (i, 0)),
        ],
        out_shape=[
            jax.ShapeDtypeStruct((dm.t_all, q_rank), BF16),
            jax.ShapeDtypeStruct((dm.t_all, kv_rank), BF16),
            jax.ShapeDtypeStruct((dm.t_all, LANES), BF16),
        ],
        compiler_params=_cparams("arbitrary"),
        name="mla_down_proj",
    )(h, wq, wkv, gq.reshape(1, -1), gkv.reshape(1, -1), cos_t, sin_t)


def _qup_kernel(a_ref, wn_ref, wr_ref, cos_ref, sin_ref, o_ref, *, qscale, heads):
    a = a_ref[...]
    acc_n = _dot(a, wn_ref[...])
    acc_r = _dot(a, wr_ref[...])
    cos, sin = cos_ref[...], sin_ref[...]
    lo_mask = lax.broadcasted_iota(jnp.int32, (1, LANES), 1) < A_HEAD_DIM
    for h in range(heads):
        o_ref[:, h * 2 * LANES: h * 2 * LANES + LANES] = (acc_n[:, h * LANES:(h + 1) * LANES] * qscale).astype(
            o_ref.dtype)
    for g in range(heads // 2):
        r = _rope128(acc_r[:, g * LANES:(g + 1) * LANES], cos, sin) * qscale
        first = (2 * g) * 2 * LANES + LANES
        second = (2 * g + 1) * 2 * LANES + LANES
        o_ref[:, first: first + LANES] = jnp.where(lo_mask, r, 0.0).astype(o_ref.dtype)
        o_ref[:, second: second + LANES] = jnp.where(lo_mask, pltpu.roll(r, A_HEAD_DIM, 1), 0.0).astype(o_ref.dtype)


def _qup_proj(dm, zq, w_nope, w_rope, cos_t, sin_t, qscale, n_heads):
    tm = dm.tm
    k = zq.shape[1]
    hb = 4
    assert n_heads % hb == 0
    rope_spec = pl.BlockSpec((tm, LANES), lambda i, j: (dm.ropeblk(i, tm), 0))
    return pl.pallas_call(
        functools.partial(_qup_kernel, qscale=qscale, heads=hb),
        grid=(dm.t_lat // tm, n_heads // hb),
        in_specs=[
            pl.BlockSpec((tm, k), lambda i, j: (i, 0)),
            pl.BlockSpec((k, hb * LANES), lambda i, j: (0, j)),
            pl.BlockSpec((k, hb * A_HEAD_DIM), lambda i, j: (0, j)),
            rope_spec, rope_spec,
        ],
        out_specs=pl.BlockSpec((tm, hb * 2 * LANES), lambda i, j: (i, j)),
        out_shape=jax.ShapeDtypeStruct((dm.t_lat, n_heads * 2 * LANES), BF16),
        compiler_params=_cparams("arbitrary", "arbitrary"),
        name="mla_q_up",
    )(zq, w_nope, w_rope, cos_t, sin_t)


def _kvup_kernel(a_ref, wk_ref, wvt_ref, kr_ref, k_ref, vt_ref, *, heads):
    a = a_ref[...]
    kn = _dot(a, wk_ref[...])
    kr = kr_ref[...]
    for h in range(heads):
        k_ref[:, h * 2 * LANES: h * 2 * LANES + LANES] = kn[:, h * LANES: (h + 1) * LANES].astype(k_ref.dtype)
        k_ref[:, h * 2 * LANES + LANES: (h + 1) * 2 * LANES] = kr
    vt_ref[...] = _dot_nt(wvt_ref[...], a).astype(vt_ref.dtype)


def _kvup_proj(dm, ckv, w_uk, w_uvt, kr, n_heads):
    tm = dm.tm
    kv_rank = ckv.shape[1]
    hb = math.gcd(8, n_heads)
    return pl.pallas_call(
        functools.partial(_kvup_kernel, heads=hb),
        grid=(dm.t_all // tm, n_heads // hb),
        in_specs=[
            pl.BlockSpec((tm, kv_rank), lambda i, j: (i, 0)),
            pl.BlockSpec((kv_rank, hb * LANES), lambda i, j: (0, j)),
            pl.BlockSpec((hb * LANES, kv_rank), lambda i, j: (j, 0)),
            pl.BlockSpec((tm, LANES), lambda i, j: (i, 0)),
        ],
        out_specs=[
            pl.BlockSpec((tm, hb * 2 * LANES), lambda i, j: (i, j)),
            pl.BlockSpec((hb * LANES, tm), lambda i, j: (j, i)),
        ],
        out_shape=[
            jax.ShapeDtypeStruct((dm.t_all, n_heads * 2 * LANES), BF16),
            jax.ShapeDtypeStruct((n_heads * LANES, dm.t_all), BF16),
        ],
        compiler_params=_cparams("arbitrary", "arbitrary"),
        name="mla_kv_up",
    )(ckv, w_uk, w_uvt, kr)


def _mla_flash_kernel(q_ref, kl_ref, kc_ref, vl_ref, vc_ref, o_ref, m_sc, l_sc, acc_sc, sa_sc, sb_sc, pa_sc, pb_sc,
                      qt_sc, *, tk, n_chunks):
    tq = q_ref.shape[0]
    half = tq // 2
    n_ctx = kc_ref.shape[0]
    s_bufs, p_bufs = (sa_sc, sb_sc), (pa_sc, pb_sc)
    qt_sc[...] = q_ref[...].T

    def rows(ci):
        return tk if ci < n_chunks else n_ctx

    def keys(ci):
        return kl_ref[ci * tk:(ci + 1) * tk, :] if ci < n_chunks else kc_ref[...]

    def values_t(ci):
        return vl_ref[:, ci * tk:(ci + 1) * tk] if ci < n_chunks else vc_ref[...]

    n_all = n_chunks + 1
    s_bufs[0][:rows(0), :] = _dot(keys(0), qt_sc[...])
    for ci in range(n_all):
        for hf in range(2):
            sl = slice(hf * half, (hf + 1) * half)
            if ci > 0:
                pv = _dot(values_t(ci - 1), p_bufs[(ci - 1) % 2][:rows(ci - 1), sl])
            if ci + 1 < n_all:
                s_bufs[(ci + 1) % 2][:rows(ci + 1), sl] = _dot(keys(ci + 1), qt_sc[:, sl])
            s = s_bufs[ci % 2][:rows(ci), sl]
            m_cur = jnp.max(s, axis=0, keepdims=True)
            if ci == 0:
                m_new = m_cur
            else:
                m_prev = m_sc[:, sl]
                m_new = jnp.maximum(m_prev, m_cur)
                alpha = jnp.exp2(m_prev - m_new)
            p = jnp.exp2(s - m_new)
            p_sum = jnp.sum(p, axis=0, keepdims=True)
            l_sc[:, sl] = p_sum if ci == 0 else alpha * l_sc[:, sl] + p_sum
            m_sc[:, sl] = m_new
            p_bufs[ci % 2][:rows(ci), sl] = p.astype(BF16)
            if ci == 1:
                acc_sc[:, sl] = alpha * pv
            elif ci > 1:
                acc_sc[:, sl] = alpha * (acc_sc[:, sl] + pv)
    pv = _dot(values_t(n_all - 1), p_bufs[(n_all - 1) % 2][:rows(n_all - 1), :])
    o_ref[...] = ((acc_sc[...] + pv) / l_sc[...]).T.astype(o_ref.dtype)


def _mla_flash(dm, q, k, vt, n_heads):
    tq = min(512, dm.s)
    tk = min(512, dm.s // 2)
    nq = dm.s // tq
    n_chunks = dm.s // tk
    assert dm.c <= tk and tq % (2 * LANES) == 0
    return pl.pallas_call(
        functools.partial(_mla_flash_kernel, tk=tk, n_chunks=n_chunks),
        grid=(dm.b, n_heads, nq),
        in_specs=[
            pl.BlockSpec((tq, 2 * LANES), lambda b, h, i: (b * nq + i, h)),
            pl.BlockSpec((dm.s, 2 * LANES), lambda b, h, i: (b, h)),
            pl.BlockSpec((dm.c, 2 * LANES), lambda b, h, i: (dm.t_lat // dm.c + b, h)),
            pl.BlockSpec((LANES, dm.s), lambda b, h, i: (h, b)),
            pl.BlockSpec((LANES, dm.c), lambda b, h, i: (h, dm.t_lat // dm.c + b)),
        ],
        out_specs=pl.BlockSpec((tq, LANES), lambda b, h, i: (b * nq + i, h)),
        out_shape=jax.ShapeDtypeStruct((dm.t_lat, n_heads * LANES), BF16),
        scratch_shapes=[
            pltpu.VMEM((1, tq), F32),
            pltpu.VMEM((1, tq), F32),
            pltpu.VMEM((LANES, tq), F32),
            pltpu.VMEM((tk, tq), F32),
            pltpu.VMEM((tk, tq), F32),
            pltpu.VMEM((tk, tq), BF16),
            pltpu.VMEM((tk, tq), BF16),
            pltpu.VMEM((2 * LANES, tq), BF16),
        ],
        compiler_params=_cparams("arbitrary", "arbitrary", "arbitrary"),
        name="mla_flash",
    )(q, k, k, vt, vt)


def _mixer_a(dm, h, w_qkv, sink, cos_t, sin_t):
    n_heads = sink.shape[0]
    n_kv = (w_qkv.shape[1] // A_HEAD_DIM - n_heads) // 2
    assert (n_heads // n_kv) % 2 == 0 and n_kv % 2 == 0
    z = _qkv_proj(dm, h, w_qkv.astype(BF16), cos_t, sin_t, n_heads * A_HEAD_DIM, n_kv * A_HEAD_DIM)
    return _attn_a(dm, z, sink, n_heads, n_kv)


def _mixer_b(dm, h, w_dkv, g_q, g_kv, w_uq, w_ukv, wo_rows, cos_t, sin_t):
    q_rank, kv_rank = g_q.shape[0], g_kv.shape[0]
    rope = w_dkv.shape[1] - q_rank - kv_rank
    n_heads = (w_uq.shape[1] - w_ukv.shape[1] + wo_rows) // rope
    nope = (w_ukv.shape[1] - wo_rows) // n_heads
    d_v = wo_rows // n_heads
    assert rope == A_HEAD_DIM and nope == LANES and d_v == LANES
    wq = w_dkv[:, :q_rank].astype(BF16)
    wkv = jnp.pad(w_dkv[:, q_rank:], ((0, 0), (0, LANES - rope))).astype(BF16)
    zq, ckv, kr = _dkv_proj(dm, h, wq, wkv, g_q, g_kv, cos_t, sin_t)
    w_uq3 = w_uq.reshape(q_rank, n_heads, nope + rope)
    w_uq_n = w_uq3[..., :nope].reshape(q_rank, n_heads * nope).astype(BF16)
    w_uq_r = w_uq3[..., nope:].reshape(q_rank, n_heads * rope).astype(BF16)
    qscale = (nope + rope) ** -0.5 * LOG2E
    q = _qup_proj(dm, zq, w_uq_n, w_uq_r, cos_t, sin_t, qscale, n_heads)
    w_ukv3 = w_ukv.reshape(kv_rank, n_heads, nope + d_v)
    w_uk = w_ukv3[..., :nope].reshape(kv_rank, n_heads * nope).astype(BF16)
    w_uvt = jnp.transpose(w_ukv3[..., nope:], (1, 2, 0)).reshape(n_heads * d_v, kv_rank).astype(BF16)
    k, vt = _kvup_proj(dm, ckv, w_uk, w_uvt, kr, n_heads)
    return _mla_flash(dm, q, k, vt, n_heads)


def kernel(x, c, ctx, c_ctx, ada_w, ada_b, norm_g, final_g, a_wqkv, a_wo, a_sink, b_wdkv, b_gq, b_gkv, b_wuq,
           b_wukv, b_wo, r_wg, r_bg, r_we, r_be, e_wgu, e_wdn):
    b, s, d = x.shape
    c_len = ctx.shape[1]
    depth = ada_w.shape[0]
    dm = _Dims(b, s, c_len, d)
    assert b + 1 <= MOD_ROWS and dm.t_lat % c_len == 0 and c_len % WINDOW == 0 and s % GRID_W == 0
    n_groups, n_experts = r_wg.shape[2], r_we.shape[2]
    assert n_groups + n_experts <= LANES

    cc = jnp.concatenate([c, c_ctx[None, :], jnp.zeros((MOD_ROWS - b - 1, d), F32)], axis=0)
    mod = _ada(cc, ada_w, ada_b)
    mod3 = [mod[i].reshape(MOD_ROWS, 1, 6 * d) for i in range(depth)]
    cos_t, sin_t = _rope_tables(s, A_HEAD_DIM, dm.tm)

    xa = (x.reshape(dm.t_lat, d), ctx.reshape(dm.t_ctx, d))
    h = xa + (norm_g[0, 0], mod3[0])
    out = None
    for i in range(depth):
        last = i == depth - 1
        j = i // 2
        rows = dm.t_lat if last else dm.t_all
        if i % 2 == 0:
            o = _mixer_a(dm, h, a_wqkv[j], a_sink[j], cos_t, sin_t)
            w_o = a_wo[j]
        else:
            o = _mixer_b(dm, h, b_wdkv[j], b_gq[j], b_gkv[j], b_wuq[j], b_wukv[j], b_wo.shape[1], cos_t, sin_t)
            w_o = b_wo[j]
        x1 = _mm_res(dm, o, w_o.astype(BF16), xa, mod3[i], 2, rows)
        w_r = jnp.pad(jnp.concatenate([r_wg[i], r_we[i]], axis=1), ((0, 0), (0, LANES - n_groups - n_experts)))
        b_r = jnp.pad(jnp.concatenate([r_bg[i], r_be[i]]), (0, LANES - n_groups - n_experts)).reshape(1, LANES)
        h2, route = _mod_router(dm, x1, norm_g[i, 1], mod3[i], 3, 4, w_r, b_r, rows, n_groups, n_experts)
        blk_e, nxt_e, src_tok, n_used, dest, n_rows = _dispatch(route, n_experts)
        y_buf = _moe_experts(h2, blk_e, nxt_e, src_tok, n_used, e_wgu, e_wdn, i, n_rows)
        if last:
            out = _combine(dm, y_buf, dest, x1, route, mod3[i], 5, rows, final_g=final_g)
        else:
            xa, h = _combine(dm, y_buf, dest, x1, route, mod3[i], 5, rows,
                             next_g=norm_g[i + 1, 0], next_mod3=mod3[i + 1])
    return out.reshape(b, s, d)
```

```python
import functools
import math

import jax
import jax.numpy as jnp
from jax import lax
from jax.experimental import pallas as pl
from jax.experimental.pallas import tpu as pltpu

F32 = jnp.float32
BF16 = jnp.bfloat16

EPS = 1e-6
NEG_INF = -1e30
ROPE_BASE = 10000.0
GRID_W = 64
A_HEAD_DIM = 64
WINDOW = 128
MOE_BLOCK = 128
LANES = 128
MOD_ROWS = 8
LOG2E = math.log2(math.e)
WEIGHT_CAST_CHUNKS = 8
BULK_DMA_PRIORITY = 1
MOD_CHUNK_ROWS = 32
VMEM_LIMIT_BYTES = 56 * 1024 * 1024


def _cparams(*sem):
    return pltpu.CompilerParams(dimension_semantics=sem, vmem_limit_bytes=VMEM_LIMIT_BYTES)


def _dot(a, b):
    return jnp.dot(a, b, preferred_element_type=F32)


def _dot_nt(a, b):
    return lax.dot_general(a, b, (((1,), (1,)), ((), ())), preferred_element_type=F32)


def _split_bf16(v):
    hi = v.astype(BF16)
    lo = (v - hi.astype(F32)).astype(BF16)
    return hi, lo


def _rms(x):
    return x * lax.rsqrt(jnp.mean(x * x, axis=-1, keepdims=True) + EPS)


def _modulate_val(x, g, shift, scale):
    return _rms(x) * g * (1.0 + scale) + shift


def _rope128(x, cos, sin_signed):
    lane = lax.broadcasted_iota(jnp.int32, x.shape, 1)
    first_half = (lane & 32) == 0
    partner = jnp.where(first_half, pltpu.roll(x, 96, 1), pltpu.roll(x, 32, 1))
    return x * cos + partner * sin_signed


def _ada_kernel(c_ref, w_ref, b_ref, o_ref):
    c = c_ref[...]
    s = c * jax.nn.sigmoid(c)
    s_hi, s_lo = _split_bf16(s)
    w_hi, w_lo = _split_bf16(w_ref[...])
    r = _dot(jnp.concatenate([s_hi, s_lo], axis=0), w_hi)
    o_ref[...] = r[:MOD_ROWS] + r[MOD_ROWS:] + _dot(s_hi, w_lo) + b_ref[...]


def _ada(cc, ada_w, ada_b):
    depth, d, n = ada_w.shape
    tn = 512
    return pl.pallas_call(
        _ada_kernel,
        grid=(depth, n // tn),
        in_specs=[
            pl.BlockSpec((MOD_ROWS, d), lambda l, j: (0, 0)),
            pl.BlockSpec((None, d, tn), lambda l, j: (l, 0, j)),
            pl.BlockSpec((None, 1, tn), lambda l, j: (l, 0, j)),
        ],
        out_specs=pl.BlockSpec((None, MOD_ROWS, tn), lambda l, j: (l, 0, j)),
        out_shape=jax.ShapeDtypeStruct((depth, MOD_ROWS, n), F32),
        compiler_params=_cparams("arbitrary", "arbitrary"),
        name="ada",
    )(cc, ada_w, ada_b.reshape(depth, 1, n))


class _Dims:
    def __init__(self, b, s, c, d):
        self.b, self.s, self.c, self.d = b, s, c, d
        self.t_lat, self.t_ctx = b * s, b * c
        self.t_all = self.t_lat + self.t_ctx
        self.tm = next(t for t in (512, 256, 128) if s % t == 0 and self.t_ctx % t == 0)

    def modrow(self, i, tm):
        return jnp.minimum(i // (self.s // tm), self.b)

    def ropeblk(self, i, tm):
        return jnp.where(i < self.t_lat // tm, i % (self.s // tm), self.s // tm)

    def mod_spec(self, col, tm, width=None):
        width = self.d if width is None else width
        per = self.d // width
        if width == self.d:
            return pl.BlockSpec((1, 1, width), lambda i, *_: (self.modrow(i, tm), 0, col))
        return pl.BlockSpec((1, 1, width), lambda i, j, *_: (self.modrow(i, tm), 0, col * per + j))


def _rope_tables(s, d_rot, pad_rows):
    rows = s // GRID_W
    row = jnp.repeat(jnp.arange(rows, dtype=F32), GRID_W)
    col = jnp.tile(jnp.arange(GRID_W, dtype=F32), rows)
    n_freq = d_rot // 4
    inv = ROPE_BASE ** (-jnp.arange(n_freq, dtype=F32) / n_freq)
    ang = jnp.concatenate([row[:, None] * inv, col[:, None] * inv], axis=-1)
    cos, sin = jnp.cos(ang), jnp.sin(ang)
    reps = LANES // d_rot
    cos_t = jnp.tile(cos, (1, 2 * reps))
    sin_t = jnp.tile(jnp.concatenate([-sin, sin], axis=-1), (1, reps))
    cos_t = jnp.concatenate([cos_t, jnp.ones((pad_rows, LANES), F32)], axis=0)
    sin_t = jnp.concatenate([sin_t, jnp.zeros((pad_rows, LANES), F32)], axis=0)
    return cos_t, sin_t


def _qkv_kernel(*refs, n_q_blocks, n_rope_blocks, qscale, n_lat_blocks, tiles, tn):
    i = pl.program_id(0)
    j = pl.program_id(1)
    if n_lat_blocks is None:
        a_ref, w_ref, cos_ref, sin_ref, o_ref = refs
    else:
        xl_ref, xc_ref, g_ref, sh_ref, sc_ref, w_ref, cos_ref, sin_ref, o_ref, a_ref = refs

        @pl.when(j == 0)
        def _():
            def mod_rows(r, carry):
                rows = pl.ds(pl.multiple_of(r * MOD_CHUNK_ROWS, MOD_CHUNK_ROWS), MOD_CHUNK_ROWS)
                x = jnp.where(i < n_lat_blocks, xl_ref[rows, :], xc_ref[rows, :])
                a_ref[rows, :] = _modulate_val(x, g_ref[...], sh_ref[0], sc_ref[0]).astype(a_ref.dtype)
                return carry

            lax.fori_loop(0, a_ref.shape[0] // MOD_CHUNK_ROWS, mod_rows, 0)
    cos, sin = cos_ref[...], sin_ref[...]
    for t in range(tiles):
        col = j * tiles + t
        acc = _dot(a_ref[...], w_ref[:, t * tn:(t + 1) * tn])
        is_rope = col < n_rope_blocks
        scale = jnp.where(col < n_q_blocks, qscale, 1.0).astype(F32)
        cos_t = jnp.where(is_rope, cos, 1.0)
        sin_t = jnp.where(is_rope, sin, 0.0)
        for g in range(tn // LANES):
            sl = slice(g * LANES, (g + 1) * LANES)
            o_ref[:, t * tn + g * LANES: t * tn + (g + 1) * LANES] = (
                _rope128(acc[:, sl], cos_t, sin_t) * scale).astype(o_ref.dtype)


def _qkv_proj(dm, h, w, cos_t, sin_t, qd, kvd):
    tm = dm.tm
    n = w.shape[1]
    tn = math.gcd(512, kvd)
    fused = isinstance(h, tuple)
    n_lat = dm.t_lat // tm
    tiles = 2 if (n // tn) % 2 == 0 else 1
    kern = functools.partial(_qkv_kernel, n_q_blocks=qd // tn, n_rope_blocks=(qd + kvd) // tn,
                             qscale=A_HEAD_DIM ** -0.5 * LOG2E,
                             n_lat_blocks=n_lat if fused else None, tiles=tiles, tn=tn)
    tail_specs = [
        pl.BlockSpec((dm.d, tiles * tn), lambda i, j: (0, j)),
        pl.BlockSpec((tm, LANES), lambda i, j: (dm.ropeblk(i, tm), 0)),
        pl.BlockSpec((tm, LANES), lambda i, j: (dm.ropeblk(i, tm), 0)),
    ]
    if fused:
        x_lat, x_ctx, gain, mod3 = h
        in_specs = [
            pl.BlockSpec((tm, dm.d), lambda i, j: (jnp.minimum(i, n_lat - 1), 0)),
            pl.BlockSpec((tm, dm.d), lambda i, j: (jnp.maximum(i - n_lat, 0), 0), pipeline_mode=pl.Buffered(1)),
            pl.BlockSpec((1, dm.d), lambda i, j: (0, 0)),
            pl.BlockSpec((1, 1, dm.d), lambda i, j: (dm.modrow(i, tm), 0, 0)),
            pl.BlockSpec((1, 1, dm.d), lambda i, j: (dm.modrow(i, tm), 0, 1)),
        ] + tail_specs
        args = (x_lat, x_ctx, gain.reshape(1, dm.d), mod3, mod3, w, cos_t, sin_t)
        scratch = [pltpu.VMEM((tm, dm.d), BF16)]
    else:
        in_specs = [pl.BlockSpec((tm, dm.d), lambda i, j: (i, 0))] + tail_specs
        args = (h, w, cos_t, sin_t)
        scratch = []
    return pl.pallas_call(
        kern,
        grid=(dm.t_all // tm, n // (tiles * tn)),
        in_specs=in_specs,
        out_specs=pl.BlockSpec((tm, tiles * tn), lambda i, j: (i, j)),
        out_shape=jax.ShapeDtypeStruct((dm.t_all, n), BF16),
        scratch_shapes=scratch,
        compiler_params=_cparams("arbitrary", "arbitrary"),
        name="qkv_proj",
    )(*args)


def _attn_a_kernel(sink_ref, q_ref, kp_ref, kc_ref, kn_ref, vp_ref, vc_ref, vn_ref, kx_ref, vx_ref, o_ref,
                   *, nb, group):
    i = pl.program_id(1)
    hp = pl.program_id(2)
    npair = group // 2
    w = WINDOW
    hd = A_HEAD_DIM
    n_loc = 3 * w
    nq = npair * w
    lane = lax.broadcasted_iota(jnp.int32, (1, LANES), 1)
    lo_mask = lane < hd
    lane_q = lax.broadcasted_iota(jnp.int32, (1, nq), 1)

    k_all = jnp.concatenate([kp_ref[...], kc_ref[...], kn_ref[...], kx_ref[...]], axis=0).astype(F32)
    v_all = jnp.concatenate([vp_ref[...], vc_ref[...], vn_ref[...], vx_ref[...]], axis=0).astype(F32)
    v_t = v_all.T
    zeros_half = jnp.zeros((hd, v_t.shape[1]), F32)

    c = lax.broadcasted_iota(jnp.int32, (n_loc, w), 0)
    r = lax.broadcasted_iota(jnp.int32, (n_loc, w), 1)
    rel = c - w - r
    valid = ((rel >= -w) & (rel <= w) & ((c >= w) | (i > 0)) & ((c < 2 * w) | (i < nb - 1)) & (i < nb))
    bias = jnp.where(valid, 0.0, NEG_INF).astype(F32)
    bias = jnp.concatenate([bias] * npair, axis=1)

    scores, values = [], []
    for par in range(2):
        if par == 0:
            k_left = jnp.where(lo_mask, k_all, 0.0)
            k_right = pltpu.roll(k_left, hd, 1)
        else:
            k_right = jnp.where(lo_mask, 0.0, k_all)
            k_left = pltpu.roll(k_right, hd, 1)
        v_h = v_t[par * hd:(par + 1) * hd]
        values.append((jnp.concatenate([v_h, zeros_half], axis=0).astype(BF16),
                       jnp.concatenate([zeros_half, v_h], axis=0).astype(BF16)))
        base = par * group * hd
        qs = jnp.concatenate([q_ref[:, base + pp * LANES: base + (pp + 1) * LANES] for pp in range(npair)], axis=0)
        qs_t = qs.T
        scores.append((_dot(k_left.astype(BF16), qs_t), _dot(k_right.astype(BF16), qs_t)))

    for par in range(2):
        base = par * group * hd
        o_pair_t = None
        for side in range(2):
            s = scores[par][side]
            head0 = (2 * hp + par) * group + side
            sink = jnp.full((1, nq), sink_ref[head0], F32)
            for pp in range(1, npair):
                sink = jnp.where(lane_q >= pp * w, sink_ref[head0 + 2 * pp], sink)
            sink = sink * LOG2E
            sl = s[:n_loc] + bias
            sc = s[n_loc:]
            m = jnp.maximum(jnp.maximum(jnp.max(sl, axis=0, keepdims=True), jnp.max(sc, axis=0, keepdims=True)),
                            sink)
            el = jnp.exp2(sl - m)
            ec = jnp.exp2(sc - m)
            den = jnp.sum(el, axis=0, keepdims=True) + jnp.sum(ec, axis=0, keepdims=True) + jnp.exp2(sink - m)
            e = jnp.concatenate([el, ec], axis=0).astype(BF16)
            o_t = _dot(values[par][side], e) / den
            o_pair_t = o_t if o_pair_t is None else o_pair_t + o_t
        o_pair = o_pair_t.T
        for pp in range(npair):
            o_ref[:, base + pp * LANES: base + (pp + 1) * LANES] = o_pair[pp * w:(pp + 1) * w].astype(o_ref.dtype)


def _attn_a(dm, z, sink, n_heads, n_kv):
    group = n_heads // n_kv
    qd, kvd = n_heads * A_HEAD_DIM, n_kv * A_HEAD_DIM
    w = WINDOW
    nb = dm.s // w
    ncb = dm.c // w
    lat_blocks = dm.t_lat // w
    qw = 2 * group * A_HEAD_DIM
    kcol, vcol = qd // LANES, (qd + kvd) // LANES

    def qrow(b, i):
        return jnp.where(i < nb, b * nb + i, lat_blocks + b * ncb + (i - nb))

    def krow(b, i, off):
        return b * nb + jnp.clip(jnp.minimum(i, nb - 1) + off, 0, nb - 1)

    def kv_spec(col, off):
        return pl.BlockSpec((w, LANES), lambda b, i, hp, s_: (krow(b, i, off), col + hp))

    def ctx_spec(col):
        return pl.BlockSpec((dm.c, LANES), lambda b, i, hp, s_: (dm.t_lat // dm.c + b, col + hp))

    grid_spec = pltpu.PrefetchScalarGridSpec(
        num_scalar_prefetch=1,
        grid=(dm.b, nb + ncb, n_kv // 2),
        in_specs=[
            pl.BlockSpec((w, qw), lambda b, i, hp, s_: (qrow(b, i), hp)),
            kv_spec(kcol, -1), kv_spec(kcol, 0), kv_spec(kcol, 1),
            kv_spec(vcol, -1), kv_spec(vcol, 0), kv_spec(vcol, 1),
            ctx_spec(kcol), ctx_spec(vcol),
        ],
        out_specs=pl.BlockSpec((w, qw), lambda b, i, hp, s_: (qrow(b, i), hp)),
    )
    return pl.pallas_call(
        functools.partial(_attn_a_kernel, nb=nb, group=group),
        grid_spec=grid_spec,
        out_shape=jax.ShapeDtypeStruct((dm.t_all, qd), BF16),
        compiler_params=_cparams("arbitrary", "arbitrary", "arbitrary"),
        name="attn_window",
    )(sink, z, z, z, z, z, z, z, z, z)


def _mm_res_kernel(a_ref, w_ref, *refs, n_lat_blocks):
    if n_lat_blocks is None:
        res_ref, gate_ref, o_ref = refs
        res = res_ref[...]
    else:
        rl_ref, rc_ref, gate_ref, o_ref = refs
        res = jnp.where(pl.program_id(0) < n_lat_blocks, rl_ref[...], rc_ref[...])
    o_ref[...] = res + gate_ref[0] * _dot(a_ref[...], w_ref[...])


def _mm_res(dm, a, w, res, mod3, gate_col, rows):
    tm = 2 * dm.tm if (rows % (2 * dm.tm) == 0 and dm.s % (2 * dm.tm) == 0) else dm.tm
    k, n = w.shape
    tn = min(512, n)
    split = isinstance(res, tuple)
    n_lat = dm.t_lat // tm
    if split:
        res_specs = [pl.BlockSpec((tm, tn), lambda i, j: (jnp.minimum(i, n_lat - 1), j)),
                     pl.BlockSpec((tm, tn), lambda i, j: (jnp.maximum(i - n_lat, 0), j))]
        res_args = res
    else:
        res_specs = [pl.BlockSpec((tm, tn), lambda i, j: (i, j))]
        res_args = (res,)
    return pl.pallas_call(
        functools.partial(_mm_res_kernel, n_lat_blocks=n_lat if split else None),
        grid=(rows // tm, n // tn),
        in_specs=[
            pl.BlockSpec((tm, k), lambda i, j: (i, 0)),
            pl.BlockSpec((k, tn), lambda i, j: (0, j)),
            *res_specs,
            dm.mod_spec(gate_col, tm, tn),
        ],
        out_specs=pl.BlockSpec((tm, tn), lambda i, j: (i, j)),
        out_shape=jax.ShapeDtypeStruct((rows, n), F32),
        compiler_params=_cparams("arbitrary", "arbitrary"),
        name="out_proj_residual",
    )(a, w, *res_args, mod3)


def _mod_router_kernel(x_ref, g_ref, sh_ref, sc_ref, wh_ref, wl_ref, br_ref, h_ref, route_ref, *, n_groups, n_experts):
    h = _modulate_val(x_ref[...], g_ref[...], sh_ref[0], sc_ref[0])
    h_ref[...] = h
    h_hi, h_lo = _split_bf16(h)
    wh = wh_ref[...]
    lg = _dot(h_hi, wh) + _dot(h_lo, wh) + _dot(h_hi, wl_ref[...]) + br_ref[...]

    per = n_experts // n_groups
    lane = lax.broadcasted_iota(jnp.int32, lg.shape, 1)
    lane_f = lane.astype(F32)
    big = float(LANES)

    def first_argmax(v, vmax):
        return jnp.min(jnp.where(v == vmax, lane_f, big), axis=1, keepdims=True)

    is_g = lane < n_groups
    g_l = jnp.where(is_g, lg, NEG_INF)
    gmax = jnp.max(g_l, axis=1, keepdims=True)
    grp = first_argmax(g_l, gmax)
    p_grp = 1.0 / jnp.sum(jnp.where(is_g, jnp.exp(g_l - gmax), 0.0), axis=1, keepdims=True)

    lo = n_groups + grp * per
    e_l = jnp.where((lane_f >= lo) & (lane_f < lo + per), lg, NEG_INF)
    t1 = jnp.max(e_l, axis=1, keepdims=True)
    i1 = first_argmax(e_l, t1)
    e_l2 = jnp.where(lane_f == i1, NEG_INF, e_l)
    t2 = jnp.max(e_l2, axis=1, keepdims=True)
    i2 = first_argmax(e_l2, t2)
    ratio = jnp.exp(t2 - t1)
    gate1 = p_grp / (1.0 + ratio)
    gate2 = gate1 * ratio
    route = jnp.where(lane == 0, i1 - n_groups,
                      jnp.where(lane == 1, i2 - n_groups,
                                jnp.where(lane == 2, gate1, jnp.where(lane == 3, gate2, 0.0))))
    route_ref[...] = route


def _mod_router(dm, x, g, mod3, shift_col, scale_col, w_r, b_r, rows, n_groups, n_experts):
    tm = min(dm.tm, 256)
    w_hi, w_lo = _split_bf16(w_r)
    kern = functools.partial(_mod_router_kernel, n_groups=n_groups, n_experts=n_experts)
    return pl.pallas_call(
        kern,
        grid=(rows // tm,),
        in_specs=[
            pl.BlockSpec((tm, dm.d), lambda i: (i, 0)),
            pl.BlockSpec((1, dm.d), lambda i: (0, 0)),
            dm.mod_spec(shift_col, tm),
            dm.mod_spec(scale_col, tm),
            pl.BlockSpec((dm.d, LANES), lambda i: (0, 0)),
            pl.BlockSpec((dm.d, LANES), lambda i: (0, 0)),
            pl.BlockSpec((1, LANES), lambda i: (0, 0)),
        ],
        out_specs=[
            pl.BlockSpec((tm, dm.d), lambda i: (i, 0)),
            pl.BlockSpec((tm, LANES), lambda i: (i, 0)),
        ],
        out_shape=[
            jax.ShapeDtypeStruct((rows, dm.d), F32),
            jax.ShapeDtypeStruct((rows, LANES), F32),
        ],
        compiler_params=_cparams("arbitrary"),
        name="modulate_router",
    )(x, g.reshape(1, dm.d), mod3, mod3, w_hi, w_lo, b_r)


def _dispatch(route, n_experts):
    eid = route[:, :2].astype(jnp.int32).reshape(-1)
    n_assign = eid.shape[0]
    order = jnp.argsort(eid).astype(jnp.int32)
    rank = jnp.argsort(order).astype(jnp.int32)
    onehot = eid[:, None] == jnp.arange(n_experts, dtype=jnp.int32)[None, :]
    counts = jnp.sum(onehot, axis=0, dtype=jnp.int32)
    padded = (counts + MOE_BLOCK - 1) // MOE_BLOCK * MOE_BLOCK
    p_end = jnp.cumsum(padded)
    p_start = p_end - padded
    c_start = jnp.cumsum(counts) - counts
    dest = rank + jnp.sum(jnp.where(onehot, (p_start - c_start)[None, :], 0), axis=1, dtype=jnp.int32)
    n_rows = (n_assign + n_experts * (MOE_BLOCK - 1) + MOE_BLOCK - 1) // MOE_BLOCK * MOE_BLOCK
    n_blocks = n_rows // MOE_BLOCK
    blk_start = jnp.arange(n_blocks, dtype=jnp.int32) * MOE_BLOCK
    blk_e = jnp.minimum(jnp.sum(p_end[None, :] <= blk_start[:, None], axis=1, dtype=jnp.int32), n_experts - 1)
    off = (blk_start - p_start[blk_e])[:, None] + jnp.arange(MOE_BLOCK, dtype=jnp.int32)[None, :]
    pos = jnp.clip(c_start[blk_e][:, None] + off, 0, n_assign - 1)
    src_tok = jnp.where(off < counts[blk_e][:, None], order[pos] // 2, 0).reshape(n_rows).astype(jnp.int32)
    n_used = (p_end[-1:] // MOE_BLOCK).astype(jnp.int32)
    ids = jnp.arange(n_experts, dtype=jnp.int32)
    later = (ids[None, :] > ids[:, None]) & (counts[None, :] > 0)
    nxt = jnp.min(jnp.where(later, ids[None, :], n_experts), axis=1)
    nxt_e = jnp.where(nxt < n_experts, nxt, -1).astype(jnp.int32)[blk_e]
    return blk_e, nxt_e, src_tok, n_used, dest, n_rows


def _moe_kernel(blk_e_ref, nxt_e_ref, src_ref, nused_ref, h_hbm, wgu_hbm, wdn_hbm, y_ref,
                xb, sem, wst_gu, wst_dn, wb_gu, wb_dn, wsem, *, d_exp, layer):
    b = pl.program_id(0)
    n_used = nused_ref[0]
    slot = b % 2
    e_cur = blk_e_ref[b]
    run_start = jnp.logical_or(b == 0, e_cur != blk_e_ref[jnp.maximum(b - 1, 0)])

    def row_copies(blk, slot_):
        return [pltpu.make_async_copy(h_hbm.at[pl.ds(src_ref[blk * MOE_BLOCK + r], 1), :],
                                      xb.at[slot_, pl.ds(r, 1), :], sem.at[slot_])
                for r in range(MOE_BLOCK)]

    def weight_copies(e):
        return (pltpu.make_async_copy(wgu_hbm.at[layer, e], wst_gu, wsem.at[0]),
                pltpu.make_async_copy(wdn_hbm.at[layer, e], wst_dn, wsem.at[1]))

    @pl.when(b == 0)
    def _():
        for cp in row_copies(0, 0):
            cp.start()
        for cp in weight_copies(e_cur):
            cp.start()

    @pl.when(b >= n_used)
    def _():
        y_ref[...] = jnp.zeros(y_ref.shape, y_ref.dtype)

    @pl.when(b < n_used)
    def _():
        @pl.when(run_start)
        def _():
            for cp in weight_copies(e_cur):
                cp.wait()
            rows_gu = wst_gu.shape[0] // WEIGHT_CAST_CHUNKS
            rows_dn = wst_dn.shape[0] // WEIGHT_CAST_CHUNKS

            def cast_chunk(ci, carry):
                r0 = pl.multiple_of(ci * rows_gu, rows_gu)
                wb_gu[pl.ds(r0, rows_gu), :] = wst_gu[pl.ds(r0, rows_gu), :].astype(BF16)
                r1 = pl.multiple_of(ci * rows_dn, rows_dn)
                wb_dn[pl.ds(r1, rows_dn), :] = wst_dn[pl.ds(r1, rows_dn), :].astype(BF16)
                return carry

            lax.fori_loop(0, WEIGHT_CAST_CHUNKS, cast_chunk, 0)
            nxt = nxt_e_ref[b]

            @pl.when(nxt >= 0)
            def _():
                for cp in weight_copies(nxt):
                    cp.start(priority=BULK_DMA_PRIORITY)

        for cp in row_copies(b, slot):
            cp.wait()

        @pl.when(b + 1 < n_used)
        def _():
            for cp in row_copies(b + 1, 1 - slot):
                cp.start()

        xrow = xb[slot].astype(BF16)
        gu = _dot(xrow, wb_gu[...])
        g, u = gu[:, :d_exp], gu[:, d_exp:]
        act = (g * jax.nn.sigmoid(g) * u).astype(BF16)
        y_ref[...] = _dot(act, wb_dn[...])


def _moe_experts(h, blk_e, nxt_e, src_tok, n_used, w_gu, w_dn, layer, n_rows):
    _, _, d, f2 = w_gu.shape
    d_exp = f2 // 2
    n_blocks = n_rows // MOE_BLOCK
    grid_spec = pltpu.PrefetchScalarGridSpec(
        num_scalar_prefetch=4,
        grid=(n_blocks,),
        in_specs=[
            pl.BlockSpec(memory_space=pl.ANY),
            pl.BlockSpec(memory_space=pl.ANY),
            pl.BlockSpec(memory_space=pl.ANY),
        ],
        out_specs=pl.BlockSpec((MOE_BLOCK, d), lambda b, be, ne, st, nu: (b, 0)),
        scratch_shapes=[
            pltpu.VMEM((2, MOE_BLOCK, d), F32),
            pltpu.SemaphoreType.DMA((2,)),
            pltpu.VMEM((d, f2), F32),
            pltpu.VMEM((d_exp, d), F32),
            pltpu.VMEM((d, f2), BF16),
            pltpu.VMEM((d_exp, d), BF16),
            pltpu.SemaphoreType.DMA((2,)),
        ],
    )
    return pl.pallas_call(
        functools.partial(_moe_kernel, d_exp=d_exp, layer=layer),
        grid_spec=grid_spec,
        out_shape=jax.ShapeDtypeStruct((n_rows, d), F32),
        compiler_params=_cparams("arbitrary"),
        name="moe_experts",
    )(blk_e, nxt_e, src_tok, n_used, h, w_gu, w_dn)


def _combine_kernel(dest_ref, y_hbm, x_ref, route_ref, g2_ref, *refs, tc, n_blocks, final):
    if final:
        fg_ref, out_ref, yb, sem = refs
    else:
        ng_ref, sh_ref, sc_ref, xo_ref, h_ref, yb, sem = refs
    t = pl.program_id(0)
    slot = t % 2

    def row_copies(blk, slot_):
        return [pltpu.make_async_copy(y_hbm.at[pl.ds(dest_ref[(blk * tc + r) * 2 + k], 1), :],
                                      yb.at[slot_, k, pl.ds(r, 1), :], sem.at[slot_])
                for r in range(tc) for k in range(2)]

    @pl.when(t == 0)
    def _():
        for cp in row_copies(0, 0):
            cp.start()

    for cp in row_copies(t, slot):
        cp.wait()

    @pl.when(t + 1 < n_blocks)
    def _():
        for cp in row_copies(t + 1, 1 - slot):
            cp.start()

    route = route_ref[...]
    y = route[:, 2:3] * yb[slot, 0] + route[:, 3:4] * yb[slot, 1]
    xn = x_ref[...] + g2_ref[0] * y
    if final:
        out_ref[...] = _rms(xn) * fg_ref[...]
    else:
        xo_ref[...] = xn
        h_ref[...] = _modulate_val(xn, ng_ref[...], sh_ref[0], sc_ref[0]).astype(h_ref.dtype)


def _combine(dm, y_buf, dest, x, route, mod3, gate_col, rows, *, final_g=None, next_g=None, next_mod3=None):
    tc = MOE_BLOCK
    n_blocks = rows // tc
    final = final_g is not None
    row_spec = pl.BlockSpec((tc, dm.d), lambda t, d_: (t, 0))
    vec_spec = pl.BlockSpec((1, dm.d), lambda t, d_: (0, 0))
    in_specs = [
        pl.BlockSpec(memory_space=pl.ANY),
        row_spec,
        pl.BlockSpec((tc, LANES), lambda t, d_: (t, 0)),
        dm.mod_spec(gate_col, tc),
    ]
    if final:
        in_specs += [vec_spec]
        args = (final_g.reshape(1, dm.d),)
        out_specs = row_spec
        out_shape = jax.ShapeDtypeStruct((rows, dm.d), F32)
    else:
        in_specs += [vec_spec, dm.mod_spec(0, tc), dm.mod_spec(1, tc)]
        args = (next_g.reshape(1, dm.d), next_mod3, next_mod3)
        out_specs = [row_spec, row_spec]
        out_shape = [jax.ShapeDtypeStruct((rows, dm.d), F32), jax.ShapeDtypeStruct((rows, dm.d), BF16)]
    grid_spec = pltpu.PrefetchScalarGridSpec(
        num_scalar_prefetch=1,
        grid=(n_blocks,),
        in_specs=in_specs,
        out_specs=out_specs,
        scratch_shapes=[
            pltpu.VMEM((2, 2, tc, dm.d), F32),
            pltpu.SemaphoreType.DMA((2,)),
        ],
    )
    return pl.pallas_call(
        functools.partial(_combine_kernel, tc=tc, n_blocks=n_blocks, final=final),
        grid_spec=grid_spec,
        out_shape=out_shape,
        compiler_params=_cparams("arbitrary"),
        name="moe_combine",
    )(dest, y_buf, x, route, mod3, *args)


def _dkv_kernel(a_ref, wq_ref, wkv_ref, gq_ref, gkv_ref, cos_ref, sin_ref, zq_ref, ckv_ref, kr_ref, *, kv_rank):
    a = a_ref[...]
    zq_ref[...] = (_rms(_dot(a, wq_ref[...])) * gq_ref[...]).astype(zq_ref.dtype)
    zkv = _dot(a, wkv_ref[...])
    ckv_ref[...] = (_rms(zkv[:, :kv_rank]) * gkv_ref[...]).astype(ckv_ref.dtype)
    kr_ref[...] = _rope128(zkv[:, kv_rank:], cos_ref[...], sin_ref[...]).astype(kr_ref.dtype)


def _dkv_proj(dm, h, wq, wkv, gq, gkv, cos_t, sin_t):
    tm = dm.tm
    q_rank = wq.shape[1]
    kv_rank = wkv.shape[1] - LANES
    rope_spec = pl.BlockSpec((tm, LANES), lambda i: (dm.ropeblk(i, tm), 0))
    return pl.pallas_call(
        functools.partial(_dkv_kernel, kv_rank=kv_rank),
        grid=(dm.t_all // tm,),
        in_specs=[
            pl.BlockSpec((tm, dm.d), lambda i: (i, 0)),
            pl.BlockSpec((dm.d, q_rank), lambda i: (0, 0)),
            pl.BlockSpec((dm.d, kv_rank + LANES), lambda i: (0, 0)),
            pl.BlockSpec((1, q_rank), lambda i: (0, 0)),
            pl.BlockSpec((1, kv_rank), lambda i: (0, 0)),
            rope_spec, rope_spec,
        ],
        out_specs=[
            pl.BlockSpec((tm, q_rank), lambda i: (i, 0)),
            pl.BlockSpec((tm, kv_rank), lambda i: (i, 0)),
            pl.BlockSpec((tm, LANES), lambda i: (i, 0)),
        ],
        out_shape=[
            jax.ShapeDtypeStruct((dm.t_all, q_rank), BF16),
            jax.ShapeDtypeStruct((dm.t_all, kv_rank), BF16),
            jax.ShapeDtypeStruct((dm.t_all, LANES), BF16),
        ],
        compiler_params=_cparams("arbitrary"),
        name="mla_down_proj",
    )(h, wq, wkv, gq.reshape(1, -1), gkv.reshape(1, -1), cos_t, sin_t)


def _qup_kernel(a_ref, wn_ref, wr_ref, cos_ref, sin_ref, o_ref, *, qscale, heads):
    a = a_ref[...]
    acc_n = _dot(a, wn_ref[...])
    acc_r = _dot(a, wr_ref[...])
    cos, sin = cos_ref[...], sin_ref[...]
    lo_mask = lax.broadcasted_iota(jnp.int32, (1, LANES), 1) < A_HEAD_DIM
    for h in range(heads):
        o_ref[:, h * 2 * LANES: h * 2 * LANES + LANES] = (acc_n[:, h * LANES:(h + 1) * LANES] * qscale).astype(
            o_ref.dtype)
    for g in range(heads // 2):
        r = _rope128(acc_r[:, g * LANES:(g + 1) * LANES], cos, sin) * qscale
        first = (2 * g) * 2 * LANES + LANES
        second = (2 * g + 1) * 2 * LANES + LANES
        o_ref[:, first: first + LANES] = jnp.where(lo_mask, r, 0.0).astype(o_ref.dtype)
        o_ref[:, second: second + LANES] = jnp.where(lo_mask, pltpu.roll(r, A_HEAD_DIM, 1), 0.0).astype(o_ref.dtype)


def _qup_proj(dm, zq, w_nope, w_rope, cos_t, sin_t, qscale, n_heads):
    tm = dm.tm
    k = zq.shape[1]
    hb = math.gcd(8, n_heads)
    assert hb % 2 == 0
    rope_spec = pl.BlockSpec((tm, LANES), lambda i, j: (dm.ropeblk(i, tm), 0))
    return pl.pallas_call(
        functools.partial(_qup_kernel, qscale=qscale, heads=hb),
        grid=(dm.t_lat // tm, n_heads // hb),
        in_specs=[
            pl.BlockSpec((tm, k), lambda i, j: (i, 0)),
            pl.BlockSpec((k, hb * LANES), lambda i, j: (0, j)),
            pl.BlockSpec((k, hb * A_HEAD_DIM), lambda i, j: (0, j)),
            rope_spec, rope_spec,
        ],
        out_specs=pl.BlockSpec((tm, hb * 2 * LANES), lambda i, j: (i, j)),
        out_shape=jax.ShapeDtypeStruct((dm.t_lat, n_heads * 2 * LANES), BF16),
        compiler_params=_cparams("arbitrary", "arbitrary"),
        name="mla_q_up",
    )(zq, w_nope, w_rope, cos_t, sin_t)


def _kvup_kernel(a_ref, wk_ref, wvt_ref, kr_ref, k_ref, vt_ref, *, heads):
    a = a_ref[...]
    kn = _dot(a, wk_ref[...])
    kr = kr_ref[...]
    for h in range(heads):
        k_ref[:, h * 2 * LANES: h * 2 * LANES + LANES] = kn[:, h * LANES: (h + 1) * LANES].astype(k_ref.dtype)
        k_ref[:, h * 2 * LANES + LANES: (h + 1) * 2 * LANES] = kr
    vt_ref[...] = _dot_nt(wvt_ref[...], a).astype(vt_ref.dtype)


def _kvup_proj(dm, ckv, w_uk, w_uvt, kr, n_heads):
    tm = dm.tm
    kv_rank = ckv.shape[1]
    hb = math.gcd(8, n_heads)
    return pl.pallas_call(
        functools.partial(_kvup_kernel, heads=hb),
        grid=(dm.t_all // tm, n_heads // hb),
        in_specs=[
            pl.BlockSpec((tm, kv_rank), lambda i, j: (i, 0)),
            pl.BlockSpec((kv_rank, hb * LANES), lambda i, j: (0, j)),
            pl.BlockSpec((hb * LANES, kv_rank), lambda i, j: (j, 0)),
            pl.BlockSpec((tm, LANES), lambda i, j: (i, 0)),
        ],
        out_specs=[
            pl.BlockSpec((tm, hb * 2 * LANES), lambda i, j: (i, j)),
            pl.BlockSpec((hb * LANES, tm), lambda i, j: (j, i)),
        ],
        out_shape=[
            jax.ShapeDtypeStruct((dm.t_all, n_heads * 2 * LANES), BF16),
            jax.ShapeDtypeStruct((n_heads * LANES, dm.t_all), BF16),
        ],
        compiler_params=_cparams("arbitrary", "arbitrary"),
        name="mla_kv_up",
    )(ckv, w_uk, w_uvt, kr)


def _mla_flash_kernel(q_ref, kl_ref, kc_ref, vl_ref, vc_ref, o_ref, m_sc, l_sc, acc_sc, sa_sc, sb_sc, pa_sc, pb_sc,
                      qt_sc, *, tk, n_chunks):
    tq = q_ref.shape[0]
    half = tq // 2
    n_ctx = kc_ref.shape[0]
    s_bufs, p_bufs = (sa_sc, sb_sc), (pa_sc, pb_sc)
    qt_sc[...] = q_ref[...].T

    def rows(ci):
        return tk if ci < n_chunks else n_ctx

    def keys(ci):
        return kl_ref[ci * tk:(ci + 1) * tk, :] if ci < n_chunks else kc_ref[...]

    def values_t(ci):
        return vl_ref[:, ci * tk:(ci + 1) * tk] if ci < n_chunks else vc_ref[...]

    n_all = n_chunks + 1
    s_bufs[0][:rows(0), :] = _dot(keys(0), qt_sc[...])
    for ci in range(n_all):
        for hf in range(2):
            sl = slice(hf * half, (hf + 1) * half)
            if ci > 0:
                pv = _dot(values_t(ci - 1), p_bufs[(ci - 1) % 2][:rows(ci - 1), sl])
            if ci + 1 < n_all:
                s_bufs[(ci + 1) % 2][:rows(ci + 1), sl] = _dot(keys(ci + 1), qt_sc[:, sl])
            s = s_bufs[ci % 2][:rows(ci), sl]
            m_cur = jnp.max(s, axis=0, keepdims=True)
            if ci == 0:
                m_new = m_cur
            else:
                m_prev = m_sc[:, sl]
                m_new = jnp.maximum(m_prev, m_cur)
                alpha = jnp.exp2(m_prev - m_new)
            p = jnp.exp2(s - m_new)
            p_sum = jnp.sum(p, axis=0, keepdims=True)
            l_sc[:, sl] = p_sum if ci == 0 else alpha * l_sc[:, sl] + p_sum
            m_sc[:, sl] = m_new
            p_bufs[ci % 2][:rows(ci), sl] = p.astype(BF16)
            if ci == 1:
                acc_sc[:, sl] = alpha * pv
            elif ci > 1:
                acc_sc[:, sl] = alpha * (acc_sc[:, sl] + pv)
    pv = _dot(values_t(n_all - 1), p_bufs[(n_all - 1) % 2][:rows(n_all - 1), :])
    o_ref[...] = ((acc_sc[...] + pv) / l_sc[...]).T.astype(o_ref.dtype)


def _mla_flash(dm, q, k, vt, n_heads):
    tq = min(512, dm.s)
    tk = min(512, dm.s // 2)
    nq = dm.s // tq
    n_chunks = dm.s // tk
    assert dm.c <= tk and tq % (2 * LANES) == 0
    return pl.pallas_call(
        functools.partial(_mla_flash_kernel, tk=tk, n_chunks=n_chunks),
        grid=(dm.b, n_heads, nq),
        in_specs=[
            pl.BlockSpec((tq, 2 * LANES), lambda b, h, i: (b * nq + i, h)),
            pl.BlockSpec((dm.s, 2 * LANES), lambda b, h, i: (b, h)),
            pl.BlockSpec((dm.c, 2 * LANES), lambda b, h, i: (dm.t_lat // dm.c + b, h)),
            pl.BlockSpec((LANES, dm.s), lambda b, h, i: (h, b)),
            pl.BlockSpec((LANES, dm.c), lambda b, h, i: (h, dm.t_lat // dm.c + b)),
        ],
        out_specs=pl.BlockSpec((tq, LANES), lambda b, h, i: (b * nq + i, h)),
        out_shape=jax.ShapeDtypeStruct((dm.t_lat, n_heads * LANES), BF16),
        scratch_shapes=[
            pltpu.VMEM((1, tq), F32),
            pltpu.VMEM((1, tq), F32),
            pltpu.VMEM((LANES, tq), F32),
            pltpu.VMEM((tk, tq), F32),
            pltpu.VMEM((tk, tq), F32),
            pltpu.VMEM((tk, tq), BF16),
            pltpu.VMEM((tk, tq), BF16),
            pltpu.VMEM((2 * LANES, tq), BF16),
        ],
        compiler_params=_cparams("arbitrary", "arbitrary", "arbitrary"),
        name="mla_flash",
    )(q, k, k, vt, vt)


def _mixer_a(dm, h, w_qkv, sink, cos_t, sin_t):
    n_heads = sink.shape[0]
    n_kv = (w_qkv.shape[1] // A_HEAD_DIM - n_heads) // 2
    assert (n_heads // n_kv) % 2 == 0 and n_kv % 2 == 0
    z = _qkv_proj(dm, h, w_qkv.astype(BF16), cos_t, sin_t, n_heads * A_HEAD_DIM, n_kv * A_HEAD_DIM)
    return _attn_a(dm, z, sink, n_heads, n_kv)


def _mixer_b(dm, h, w_dkv, g_q, g_kv, w_uq, w_ukv, wo_rows, cos_t, sin_t):
    q_rank, kv_rank = g_q.shape[0], g_kv.shape[0]
    rope = w_dkv.shape[1] - q_rank - kv_rank
    n_heads = (w_uq.shape[1] - w_ukv.shape[1] + wo_rows) // rope
    nope = (w_ukv.shape[1] - wo_rows) // n_heads
    d_v = wo_rows // n_heads
    assert rope == A_HEAD_DIM and nope == LANES and d_v == LANES
    wq = w_dkv[:, :q_rank].astype(BF16)
    wkv = jnp.pad(w_dkv[:, q_rank:], ((0, 0), (0, LANES - rope))).astype(BF16)
    zq, ckv, kr = _dkv_proj(dm, h, wq, wkv, g_q, g_kv, cos_t, sin_t)
    w_uq3 = w_uq.reshape(q_rank, n_heads, nope + rope)
    w_uq_n = w_uq3[..., :nope].reshape(q_rank, n_heads * nope).astype(BF16)
    w_uq_r = w_uq3[..., nope:].reshape(q_rank, n_heads * rope).astype(BF16)
    qscale = (nope + rope) ** -0.5 * LOG2E
    q = _qup_proj(dm, zq, w_uq_n, w_uq_r, cos_t, sin_t, qscale, n_heads)
    w_ukv3 = w_ukv.reshape(kv_rank, n_heads, nope + d_v)
    w_uk = w_ukv3[..., :nope].reshape(kv_rank, n_heads * nope).astype(BF16)
    w_uvt = jnp.transpose(w_ukv3[..., nope:], (1, 2, 0)).reshape(n_heads * d_v, kv_rank).astype(BF16)
    k, vt = _kvup_proj(dm, ckv, w_uk, w_uvt, kr, n_heads)
    return _mla_flash(dm, q, k, vt, n_heads)


def kernel(x, c, ctx, c_ctx, ada_w, ada_b, norm_g, final_g, a_wqkv, a_wo, a_sink, b_wdkv, b_gq, b_gkv, b_wuq,
           b_wukv, b_wo, r_wg, r_bg, r_we, r_be, e_wgu, e_wdn):
    b, s, d = x.shape
    c_len = ctx.shape[1]
    depth = ada_w.shape[0]
    dm = _Dims(b, s, c_len, d)
    assert b + 1 <= MOD_ROWS and dm.t_lat % c_len == 0 and c_len % WINDOW == 0 and s % GRID_W == 0
    n_groups, n_experts = r_wg.shape[2], r_we.shape[2]
    assert n_groups + n_experts <= LANES

    cc = jnp.concatenate([c, c_ctx[None, :], jnp.zeros((MOD_ROWS - b - 1, d), F32)], axis=0)
    mod = _ada(cc, ada_w, ada_b)
    mod3 = [mod[i].reshape(MOD_ROWS, 1, 6 * d) for i in range(depth)]
    cos_t, sin_t = _rope_tables(s, A_HEAD_DIM, dm.tm)

    xa = (x.reshape(dm.t_lat, d), ctx.reshape(dm.t_ctx, d))
    h = xa + (norm_g[0, 0], mod3[0])
    out = None
    for i in range(depth):
        last = i == depth - 1
        j = i // 2
        rows = dm.t_lat if last else dm.t_all
        if i % 2 == 0:
            o = _mixer_a(dm, h, a_wqkv[j], a_sink[j], cos_t, sin_t)
            w_o = a_wo[j]
        else:
            o = _mixer_b(dm, h, b_wdkv[j], b_gq[j], b_gkv[j], b_wuq[j], b_wukv[j], b_wo.shape[1], cos_t, sin_t)
            w_o = b_wo[j]
        x1 = _mm_res(dm, o, w_o.astype(BF16), xa, mod3[i], 2, rows)
        w_r = jnp.pad(jnp.concatenate([r_wg[i], r_we[i]], axis=1), ((0, 0), (0, LANES - n_groups - n_experts)))
        b_r = jnp.pad(jnp.concatenate([r_bg[i], r_be[i]]), (0, LANES - n_groups - n_experts)).reshape(1, LANES)
        h2, route = _mod_router(dm, x1, norm_g[i, 1], mod3[i], 3, 4, w_r, b_r, rows, n_groups, n_experts)
        blk_e, nxt_e, src_tok, n_used, dest, n_rows = _dispatch(route, n_experts)
        y_buf = _moe_experts(h2, blk_e, nxt_e, src_tok, n_used, e_wgu, e_wdn, i, n_rows)
        if last:
            out = _combine(dm, y_buf, dest, x1, route, mod3[i], 5, rows, final_g=final_g)
        else:
            xa, h = _combine(dm, y_buf, dest, x1, route, mod3[i], 5, rows,
                             next_g=norm_g[i + 1, 0], next_mod3=mod3[i + 1])
    return out.reshape(b, s, d)
```

```python
import functools
import math

import jax
import jax.numpy as jnp
from jax import lax
from jax.experimental import pallas as pl
from jax.experimental.pallas import tpu as pltpu

F32 = jnp.float32
BF16 = jnp.bfloat16

EPS = 1e-6
NEG_INF = -1e30
ROPE_BASE = 10000.0
GRID_W = 64
A_HEAD_DIM = 64
WINDOW = 128
MOE_BLOCK = 128
LANES = 128
MOD_ROWS = 8
LOG2E = math.log2(math.e)
WEIGHT_CAST_CHUNKS = 8
BULK_DMA_PRIORITY = 1
MOD_CHUNK_ROWS = 32
VMEM_LIMIT_BYTES = 56 * 1024 * 1024


def _cparams(*sem):
    return pltpu.CompilerParams(dimension_semantics=sem, vmem_limit_bytes=VMEM_LIMIT_BYTES)


def _dot(a, b):
    return jnp.dot(a, b, preferred_element_type=F32)


def _dot_nt(a, b):
    return lax.dot_general(a, b, (((1,), (1,)), ((), ())), preferred_element_type=F32)


def _split_bf16(v):
    hi = v.astype(BF16)
    lo = (v - hi.astype(F32)).astype(BF16)
    return hi, lo


def _rms(x):
    return x * lax.rsqrt(jnp.mean(x * x, axis=-1, keepdims=True) + EPS)


def _modulate_val(x, g, shift, scale):
    return _rms(x) * g * (1.0 + scale) + shift


def _rope128(x, cos, sin_signed):
    lane = lax.broadcasted_iota(jnp.int32, x.shape, 1)
    first_half = (lane & 32) == 0
    partner = jnp.where(first_half, pltpu.roll(x, 96, 1), pltpu.roll(x, 32, 1))
    return x * cos + partner * sin_signed


def _ada_kernel(c_ref, w_ref, b_ref, o_ref):
    c = c_ref[...]
    s = c * jax.nn.sigmoid(c)
    s_hi, s_lo = _split_bf16(s)
    w_hi, w_lo = _split_bf16(w_ref[...])
    r = _dot(jnp.concatenate([s_hi, s_lo], axis=0), w_hi)
    o_ref[...] = r[:MOD_ROWS] + r[MOD_ROWS:] + _dot(s_hi, w_lo) + b_ref[...]


def _ada(cc, ada_w, ada_b):
    depth, d, n = ada_w.shape
    tn = 512
    return pl.pallas_call(
        _ada_kernel,
        grid=(depth, n // tn),
        in_specs=[
            pl.BlockSpec((MOD_ROWS, d), lambda l, j: (0, 0)),
            pl.BlockSpec((None, d, tn), lambda l, j: (l, 0, j)),
            pl.BlockSpec((None, 1, tn), lambda l, j: (l, 0, j)),
        ],
        out_specs=pl.BlockSpec((None, MOD_ROWS, tn), lambda l, j: (l, 0, j)),
        out_shape=jax.ShapeDtypeStruct((depth, MOD_ROWS, n), F32),
        compiler_params=_cparams("arbitrary", "arbitrary"),
        name="ada",
    )(cc, ada_w, ada_b.reshape(depth, 1, n))


class _Dims:
    def __init__(self, b, s, c, d):
        self.b, self.s, self.c, self.d = b, s, c, d
        self.t_lat, self.t_ctx = b * s, b * c
        self.t_all = self.t_lat + self.t_ctx
        self.tm = next(t for t in (512, 256, 128) if s % t == 0 and self.t_ctx % t == 0)

    def modrow(self, i, tm):
        return jnp.minimum(i // (self.s // tm), self.b)

    def ropeblk(self, i, tm):
        return jnp.where(i < self.t_lat // tm, i % (self.s // tm), self.s // tm)

    def mod_spec(self, col, tm, width=None):
        width = self.d if width is None else width
        per = self.d // width
        if width == self.d:
            return pl.BlockSpec((1, 1, width), lambda i, *_: (self.modrow(i, tm), 0, col))
        return pl.BlockSpec((1, 1, width), lambda i, j, *_: (self.modrow(i, tm), 0, col * per + j))


def _rope_tables(s, d_rot, pad_rows):
    rows = s // GRID_W
    row = jnp.repeat(jnp.arange(rows, dtype=F32), GRID_W)
    col = jnp.tile(jnp.arange(GRID_W, dtype=F32), rows)
    n_freq = d_rot // 4
    inv = ROPE_BASE ** (-jnp.arange(n_freq, dtype=F32) / n_freq)
    ang = jnp.concatenate([row[:, None] * inv, col[:, None] * inv], axis=-1)
    cos, sin = jnp.cos(ang), jnp.sin(ang)
    reps = LANES // d_rot
    cos_t = jnp.tile(cos, (1, 2 * reps))
    sin_t = jnp.tile(jnp.concatenate([-sin, sin], axis=-1), (1, reps))
    cos_t = jnp.concatenate([cos_t, jnp.ones((pad_rows, LANES), F32)], axis=0)
    sin_t = jnp.concatenate([sin_t, jnp.zeros((pad_rows, LANES), F32)], axis=0)
    return cos_t, sin_t


def _qkv_kernel(*refs, n_q_blocks, n_rope_blocks, qscale, n_lat_blocks, tiles, tn):
    i = pl.program_id(0)
    j = pl.program_id(1)
    if n_lat_blocks is None:
        a_ref, w_ref, cos_ref, sin_ref, o_ref = refs
    else:
        xl_ref, xc_ref, g_ref, sh_ref, sc_ref, w_ref, cos_ref, sin_ref, o_ref, a_ref = refs

        @pl.when(j == 0)
        def _():
            def mod_rows(r, carry):
                rows = pl.ds(pl.multiple_of(r * MOD_CHUNK_ROWS, MOD_CHUNK_ROWS), MOD_CHUNK_ROWS)
                x = jnp.where(i < n_lat_blocks, xl_ref[rows, :], xc_ref[rows, :])
                a_ref[rows, :] = _modulate_val(x, g_ref[...], sh_ref[0], sc_ref[0]).astype(a_ref.dtype)
                return carry

            lax.fori_loop(0, a_ref.shape[0] // MOD_CHUNK_ROWS, mod_rows, 0)
    cos, sin = cos_ref[...], sin_ref[...]
    for t in range(tiles):
        col = j * tiles + t
        acc = _dot(a_ref[...], w_ref[:, t * tn:(t + 1) * tn])
        is_rope = col < n_rope_blocks
        scale = jnp.where(col < n_q_blocks, qscale, 1.0).astype(F32)
        cos_t = jnp.where(is_rope, cos, 1.0)
        sin_t = jnp.where(is_rope, sin, 0.0)
        for g in range(tn // LANES):
            sl = slice(g * LANES, (g + 1) * LANES)
            o_ref[:, t * tn + g * LANES: t * tn + (g + 1) * LANES] = (
                _rope128(acc[:, sl], cos_t, sin_t) * scale).astype(o_ref.dtype)


def _qkv_proj(dm, h, w, cos_t, sin_t, qd, kvd):
    tm = dm.tm
    n = w.shape[1]
    tn = math.gcd(512, kvd)
    fused = isinstance(h, tuple)
    n_lat = dm.t_lat // tm
    tiles = 2 if (n // tn) % 2 == 0 else 1
    kern = functools.partial(_qkv_kernel, n_q_blocks=qd // tn, n_rope_blocks=(qd + kvd) // tn,
                             qscale=A_HEAD_DIM ** -0.5 * LOG2E,
                             n_lat_blocks=n_lat if fused else None, tiles=tiles, tn=tn)
    tail_specs = [
        pl.BlockSpec((dm.d, tiles * tn), lambda i, j: (0, j)),
        pl.BlockSpec((tm, LANES), lambda i, j: (dm.ropeblk(i, tm), 0)),
        pl.BlockSpec((tm, LANES), lambda i, j: (dm.ropeblk(i, tm), 0)),
    ]
    if fused:
        x_lat, x_ctx, gain, mod3 = h
        in_specs = [
            pl.BlockSpec((tm, dm.d), lambda i, j: (jnp.minimum(i, n_lat - 1), 0)),
            pl.BlockSpec((tm, dm.d), lambda i, j: (jnp.maximum(i - n_lat, 0), 0), pipeline_mode=pl.Buffered(1)),
            pl.BlockSpec((1, dm.d), lambda i, j: (0, 0)),
            pl.BlockSpec((1, 1, dm.d), lambda i, j: (dm.modrow(i, tm), 0, 0)),
            pl.BlockSpec((1, 1, dm.d), lambda i, j: (dm.modrow(i, tm), 0, 1)),
        ] + tail_specs
        args = (x_lat, x_ctx, gain.reshape(1, dm.d), mod3, mod3, w, cos_t, sin_t)
        scratch = [pltpu.VMEM((tm, dm.d), BF16)]
    else:
        in_specs = [pl.BlockSpec((tm, dm.d), lambda i, j: (i, 0))] + tail_specs
        args = (h, w, cos_t, sin_t)
        scratch = []
    return pl.pallas_call(
        kern,
        grid=(dm.t_all // tm, n // (tiles * tn)),
        in_specs=in_specs,
        out_specs=pl.BlockSpec((tm, tiles * tn), lambda i, j: (i, j)),
        out_shape=jax.ShapeDtypeStruct((dm.t_all, n), BF16),
        scratch_shapes=scratch,
        compiler_params=_cparams("arbitrary", "arbitrary"),
        name="qkv_proj",
    )(*args)


def _attn_a_kernel(sink_ref, q_ref, kp_ref, kc_ref, kn_ref, vp_ref, vc_ref, vn_ref, kx_ref, vx_ref, o_ref,
                   *, nb, group):
    i = pl.program_id(1)
    hp = pl.program_id(2)
    npair = group // 2
    w = WINDOW
    hd = A_HEAD_DIM
    n_loc = 3 * w
    nq = npair * w
    lane = lax.broadcasted_iota(jnp.int32, (1, LANES), 1)
    lo_mask = lane < hd
    lane_q = lax.broadcasted_iota(jnp.int32, (1, nq), 1)

    k_all = jnp.concatenate([kp_ref[...], kc_ref[...], kn_ref[...], kx_ref[...]], axis=0).astype(F32)
    v_all = jnp.concatenate([vp_ref[...], vc_ref[...], vn_ref[...], vx_ref[...]], axis=0).astype(F32)
    v_t = v_all.T
    zeros_half = jnp.zeros((hd, v_t.shape[1]), F32)

    c = lax.broadcasted_iota(jnp.int32, (n_loc, w), 0)
    r = lax.broadcasted_iota(jnp.int32, (n_loc, w), 1)
    rel = c - w - r
    valid = ((rel >= -w) & (rel <= w) & ((c >= w) | (i > 0)) & ((c < 2 * w) | (i < nb - 1)) & (i < nb))
    bias = jnp.where(valid, 0.0, NEG_INF).astype(F32)
    bias = jnp.concatenate([bias] * npair, axis=1)

    scores, values = [], []
    for par in range(2):
        if par == 0:
            k_left = jnp.where(lo_mask, k_all, 0.0)
            k_right = pltpu.roll(k_left, hd, 1)
        else:
            k_right = jnp.where(lo_mask, 0.0, k_all)
            k_left = pltpu.roll(k_right, hd, 1)
        v_h = v_t[par * hd:(par + 1) * hd]
        values.append((jnp.concatenate([v_h, zeros_half], axis=0).astype(BF16),
                       jnp.concatenate([zeros_half, v_h], axis=0).astype(BF16)))
        base = par * group * hd
        qs = jnp.concatenate([q_ref[:, base + pp * LANES: base + (pp + 1) * LANES] for pp in range(npair)], axis=0)
        qs_t = qs.T
        scores.append((_dot(k_left.astype(BF16), qs_t), _dot(k_right.astype(BF16), qs_t)))

    for par in range(2):
        base = par * group * hd
        o_pair_t = None
        for side in range(2):
            s = scores[par][side]
            head0 = (2 * hp + par) * group + side
            sink = jnp.full((1, nq), sink_ref[head0], F32)
            for pp in range(1, npair):
                sink = jnp.where(lane_q >= pp * w, sink_ref[head0 + 2 * pp], sink)
            sink = sink * LOG2E
            sl = s[:n_loc] + bias
            sc = s[n_loc:]
            m = jnp.maximum(jnp.maximum(jnp.max(sl, axis=0, keepdims=True), jnp.max(sc, axis=0, keepdims=True)),
                            sink)
            el = jnp.exp2(sl - m)
            ec = jnp.exp2(sc - m)
            den = jnp.sum(el, axis=0, keepdims=True) + jnp.sum(ec, axis=0, keepdims=True) + jnp.exp2(sink - m)
            e = jnp.concatenate([el, ec], axis=0).astype(BF16)
            o_t = _dot(values[par][side], e) / den
            o_pair_t = o_t if o_pair_t is None else o_pair_t + o_t
        o_pair = o_pair_t.T
        for pp in range(npair):
            o_ref[:, base + pp * LANES: base + (pp + 1) * LANES] = o_pair[pp * w:(pp + 1) * w].astype(o_ref.dtype)


def _attn_a(dm, z, sink, n_heads, n_kv):
    group = n_heads // n_kv
    qd, kvd = n_heads * A_HEAD_DIM, n_kv * A_HEAD_DIM
    w = WINDOW
    nb = dm.s // w
    ncb = dm.c // w
    lat_blocks = dm.t_lat // w
    qw = 2 * group * A_HEAD_DIM
    kcol, vcol = qd // LANES, (qd + kvd) // LANES

    def qrow(b, i):
        return jnp.where(i < nb, b * nb + i, lat_blocks + b * ncb + (i - nb))

    def krow(b, i, off):
        return b * nb + jnp.clip(jnp.minimum(i, nb - 1) + off, 0, nb - 1)

    def kv_spec(col, off):
        return pl.BlockSpec((w, LANES), lambda b, i, hp, s_: (krow(b, i, off), col + hp))

    def ctx_spec(col):
        return pl.BlockSpec((dm.c, LANES), lambda b, i, hp, s_: (dm.t_lat // dm.c + b, col + hp))

    grid_spec = pltpu.PrefetchScalarGridSpec(
        num_scalar_prefetch=1,
        grid=(dm.b, nb + ncb, n_kv // 2),
        in_specs=[
            pl.BlockSpec((w, qw), lambda b, i, hp, s_: (qrow(b, i), hp)),
            kv_spec(kcol, -1), kv_spec(kcol, 0), kv_spec(kcol, 1),
            kv_spec(vcol, -1), kv_spec(vcol, 0), kv_spec(vcol, 1),
            ctx_spec(kcol), ctx_spec(vcol),
        ],
        out_specs=pl.BlockSpec((w, qw), lambda b, i, hp, s_: (qrow(b, i), hp)),
    )
    return pl.pallas_call(
        functools.partial(_attn_a_kernel, nb=nb, group=group),
        grid_spec=grid_spec,
        out_shape=jax.ShapeDtypeStruct((dm.t_all, qd), BF16),
        compiler_params=_cparams("arbitrary", "arbitrary", "arbitrary"),
        name="attn_window",
    )(sink, z, z, z, z, z, z, z, z, z)


def _mm_res_kernel(a_ref, w_ref, *refs, n_lat_blocks):
    if n_lat_blocks is None:
        res_ref, gate_ref, o_ref = refs
        res = res_ref[...]
    else:
        rl_ref, rc_ref, gate_ref, o_ref = refs
        res = jnp.where(pl.program_id(0) < n_lat_blocks, rl_ref[...], rc_ref[...])
    o_ref[...] = res + gate_ref[0] * _dot(a_ref[...], w_ref[...])


def _mm_res(dm, a, w, res, mod3, gate_col, rows):
    tm = 2 * dm.tm if (rows % (2 * dm.tm) == 0 and dm.s % (2 * dm.tm) == 0) else dm.tm
    k, n = w.shape
    tn = min(512, n)
    split = isinstance(res, tuple)
    n_lat = dm.t_lat // tm
    if split:
        res_specs = [pl.BlockSpec((tm, tn), lambda i, j: (jnp.minimum(i, n_lat - 1), j)),
                     pl.BlockSpec((tm, tn), lambda i, j: (jnp.maximum(i - n_lat, 0), j))]
        res_args = res
    else:
        res_specs = [pl.BlockSpec((tm, tn), lambda i, j: (i, j))]
        res_args = (res,)
    return pl.pallas_call(
        functools.partial(_mm_res_kernel, n_lat_blocks=n_lat if split else None),
        grid=(rows // tm, n // tn),
        in_specs=[
            pl.BlockSpec((tm, k), lambda i, j: (i, 0)),
            pl.BlockSpec((k, tn), lambda i, j: (0, j)),
            *res_specs,
            dm.mod_spec(gate_col, tm, tn),
        ],
        out_specs=pl.BlockSpec((tm, tn), lambda i, j: (i, j)),
        out_shape=jax.ShapeDtypeStruct((rows, n), F32),
        compiler_params=_cparams("arbitrary", "arbitrary"),
        name="out_proj_residual",
    )(a, w, *res_args, mod3)


def _mod_router_kernel(x_ref, g_ref, sh_ref, sc_ref, wh_ref, wl_ref, br_ref, h_ref, route_ref, *, n_groups, n_experts):
    h = _modulate_val(x_ref[...], g_ref[...], sh_ref[0], sc_ref[0])
    h_ref[...] = h
    h_hi, h_lo = _split_bf16(h)
    wh = wh_ref[...]
    lg = _dot(h_hi, wh) + _dot(h_lo, wh) + _dot(h_hi, wl_ref[...]) + br_ref[...]

    per = n_experts // n_groups
    lane = lax.broadcasted_iota(jnp.int32, lg.shape, 1)
    lane_f = lane.astype(F32)
    big = float(LANES)

    def first_argmax(v, vmax):
        return jnp.min(jnp.where(v == vmax, lane_f, big), axis=1, keepdims=True)

    is_g = lane < n_groups
    g_l = jnp.where(is_g, lg, NEG_INF)
    gmax = jnp.max(g_l, axis=1, keepdims=True)
    grp = first_argmax(g_l, gmax)
    p_grp = 1.0 / jnp.sum(jnp.where(is_g, jnp.exp(g_l - gmax), 0.0), axis=1, keepdims=True)

    lo = n_groups + grp * per
    e_l = jnp.where((lane_f >= lo) & (lane_f < lo + per), lg, NEG_INF)
    t1 = jnp.max(e_l, axis=1, keepdims=True)
    i1 = first_argmax(e_l, t1)
    e_l2 = jnp.where(lane_f == i1, NEG_INF, e_l)
    t2 = jnp.max(e_l2, axis=1, keepdims=True)
    i2 = first_argmax(e_l2, t2)
    ratio = jnp.exp(t2 - t1)
    gate1 = p_grp / (1.0 + ratio)
    gate2 = gate1 * ratio
    route = jnp.where(lane == 0, i1 - n_groups,
                      jnp.where(lane == 1, i2 - n_groups,
                                jnp.where(lane == 2, gate1, jnp.where(lane == 3, gate2, 0.0))))
    route_ref[...] = route


def _mod_router(dm, x, g, mod3, shift_col, scale_col, w_r, b_r, rows, n_groups, n_experts):
    tm = min(dm.tm, 256)
    w_hi, w_lo = _split_bf16(w_r)
    kern = functools.partial(_mod_router_kernel, n_groups=n_groups, n_experts=n_experts)
    return pl.pallas_call(
        kern,
        grid=(rows // tm,),
        in_specs=[
            pl.BlockSpec((tm, dm.d), lambda i: (i, 0)),
            pl.BlockSpec((1, dm.d), lambda i: (0, 0)),
            dm.mod_spec(shift_col, tm),
            dm.mod_spec(scale_col, tm),
            pl.BlockSpec((dm.d, LANES), lambda i: (0, 0)),
            pl.BlockSpec((dm.d, LANES), lambda i: (0, 0)),
            pl.BlockSpec((1, LANES), lambda i: (0, 0)),
        ],
        out_specs=[
            pl.BlockSpec((tm, dm.d), lambda i: (i, 0)),
            pl.BlockSpec((tm, LANES), lambda i: (i, 0)),
        ],
        out_shape=[
            jax.ShapeDtypeStruct((rows, dm.d), F32),
            jax.ShapeDtypeStruct((rows, LANES), F32),
        ],
        compiler_params=_cparams("arbitrary"),
        name="modulate_router",
    )(x, g.reshape(1, dm.d), mod3, mod3, w_hi, w_lo, b_r)


def _dispatch(route, n_experts):
    eid = route[:, :2].astype(jnp.int32).reshape(-1)
    n_assign = eid.shape[0]
    order = jnp.argsort(eid).astype(jnp.int32)
    rank = jnp.argsort(order).astype(jnp.int32)
    onehot = eid[:, None] == jnp.arange(n_experts, dtype=jnp.int32)[None, :]
    counts = jnp.sum(onehot, axis=0, dtype=jnp.int32)
    padded = (counts + MOE_BLOCK - 1) // MOE_BLOCK * MOE_BLOCK
    p_end = jnp.cumsum(padded)
    p_start = p_end - padded
    c_start = jnp.cumsum(counts) - counts
    dest = rank + jnp.sum(jnp.where(onehot, (p_start - c_start)[None, :], 0), axis=1, dtype=jnp.int32)
    n_rows = (n_assign + n_experts * (MOE_BLOCK - 1) + MOE_BLOCK - 1) // MOE_BLOCK * MOE_BLOCK
    n_blocks = n_rows // MOE_BLOCK
    blk_start = jnp.arange(n_blocks, dtype=jnp.int32) * MOE_BLOCK
    blk_e = jnp.minimum(jnp.sum(p_end[None, :] <= blk_start[:, None], axis=1, dtype=jnp.int32), n_experts - 1)
    off = (blk_start - p_start[blk_e])[:, None] + jnp.arange(MOE_BLOCK, dtype=jnp.int32)[None, :]
    pos = jnp.clip(c_start[blk_e][:, None] + off, 0, n_assign - 1)
    src_tok = jnp.where(off < counts[blk_e][:, None], order[pos] // 2, 0).reshape(n_rows).astype(jnp.int32)
    n_used = (p_end[-1:] // MOE_BLOCK).astype(jnp.int32)
    ids = jnp.arange(n_experts, dtype=jnp.int32)
    later = (ids[None, :] > ids[:, None]) & (counts[None, :] > 0)
    nxt = jnp.min(jnp.where(later, ids[None, :], n_experts), axis=1)
    nxt_e = jnp.where(nxt < n_experts, nxt, -1).astype(jnp.int32)[blk_e]
    return blk_e, nxt_e, src_tok, n_used, dest, n_rows


def _moe_kernel(blk_e_ref, nxt_e_ref, src_ref, nused_ref, h_hbm, wgu_hbm, wdn_hbm, y_ref,
                xb, sem, wst_gu, wst_dn, wb_gu, wb_dn, wsem, *, d_exp, layer):
    b = pl.program_id(0)
    n_used = nused_ref[0]
    slot = b % 2
    e_cur = blk_e_ref[b]
    run_start = jnp.logical_or(b == 0, e_cur != blk_e_ref[jnp.maximum(b - 1, 0)])

    def row_copies(blk, slot_):
        return [pltpu.make_async_copy(h_hbm.at[pl.ds(src_ref[blk * MOE_BLOCK + r], 1), :],
                                      xb.at[slot_, pl.ds(r, 1), :], sem.at[slot_])
                for r in range(MOE_BLOCK)]

    def weight_copies(e):
        return (pltpu.make_async_copy(wgu_hbm.at[layer, e], wst_gu, wsem.at[0]),
                pltpu.make_async_copy(wdn_hbm.at[layer, e], wst_dn, wsem.at[1]))

    @pl.when(b == 0)
    def _():
        for cp in row_copies(0, 0):
            cp.start()
        for cp in weight_copies(e_cur):
            cp.start()

    @pl.when(b >= n_used)
    def _():
        y_ref[...] = jnp.zeros(y_ref.shape, y_ref.dtype)

    @pl.when(b < n_used)
    def _():
        @pl.when(run_start)
        def _():
            for cp in weight_copies(e_cur):
                cp.wait()
            rows_gu = wst_gu.shape[0] // WEIGHT_CAST_CHUNKS
            rows_dn = wst_dn.shape[0] // WEIGHT_CAST_CHUNKS

            def cast_chunk(ci, carry):
                r0 = pl.multiple_of(ci * rows_gu, rows_gu)
                wb_gu[pl.ds(r0, rows_gu), :] = wst_gu[pl.ds(r0, rows_gu), :].astype(BF16)
                r1 = pl.multiple_of(ci * rows_dn, rows_dn)
                wb_dn[pl.ds(r1, rows_dn), :] = wst_dn[pl.ds(r1, rows_dn), :].astype(BF16)
                return carry

            lax.fori_loop(0, WEIGHT_CAST_CHUNKS, cast_chunk, 0)
            nxt = nxt_e_ref[b]

            @pl.when(nxt >= 0)
            def _():
                for cp in weight_copies(nxt):
                    cp.start(priority=BULK_DMA_PRIORITY)

        for cp in row_copies(b, slot):
            cp.wait()

        @pl.when(b + 1 < n_used)
        def _():
            for cp in row_copies(b + 1, 1 - slot):
                cp.start()

        xrow = xb[slot].astype(BF16)
        gu = _dot(xrow, wb_gu[...])
        g, u = gu[:, :d_exp], gu[:, d_exp:]
        act = (g * jax.nn.sigmoid(g) * u).astype(BF16)
        y_ref[...] = _dot(act, wb_dn[...])


def _moe_experts(h, blk_e, nxt_e, src_tok, n_used, w_gu, w_dn, layer, n_rows):
    _, _, d, f2 = w_gu.shape
    d_exp = f2 // 2
    n_blocks = n_rows // MOE_BLOCK
    grid_spec = pltpu.PrefetchScalarGridSpec(
        num_scalar_prefetch=4,
        grid=(n_blocks,),
        in_specs=[
            pl.BlockSpec(memory_space=pl.ANY),
            pl.BlockSpec(memory_space=pl.ANY),
            pl.BlockSpec(memory_space=pl.ANY),
        ],
        out_specs=pl.BlockSpec((MOE_BLOCK, d), lambda b, be, ne, st, nu: (b, 0)),
        scratch_shapes=[
            pltpu.VMEM((2, MOE_BLOCK, d), F32),
            pltpu.SemaphoreType.DMA((2,)),
            pltpu.VMEM((d, f2), F32),
            pltpu.VMEM((d_exp, d), F32),
            pltpu.VMEM((d, f2), BF16),
            pltpu.VMEM((d_exp, d), BF16),
            pltpu.SemaphoreType.DMA((2,)),
        ],
    )
    return pl.pallas_call(
        functools.partial(_moe_kernel, d_exp=d_exp, layer=layer),
        grid_spec=grid_spec,
        out_shape=jax.ShapeDtypeStruct((n_rows, d), F32),
        compiler_params=_cparams("arbitrary"),
        name="moe_experts",
    )(blk_e, nxt_e, src_tok, n_used, h, w_gu, w_dn)


def _combine_kernel(dest_ref, y_hbm, x_ref, route_ref, g2_ref, *refs, tc, n_blocks, final):
    if final:
        fg_ref, out_ref, yb, sem = refs
    else:
        ng_ref, sh_ref, sc_ref, xo_ref, h_ref, yb, sem = refs
    t = pl.program_id(0)
    slot = t % 2

    def row_copies(blk, slot_):
        return [pltpu.make_async_copy(y_hbm.at[pl.ds(dest_ref[(blk * tc + r) * 2 + k], 1), :],
                                      yb.at[slot_, k, pl.ds(r, 1), :], sem.at[slot_])
                for r in range(tc) for k in range(2)]

    @pl.when(t == 0)
    def _():
        for cp in row_copies(0, 0):
            cp.start()

    for cp in row_copies(t, slot):
        cp.wait()

    @pl.when(t + 1 < n_blocks)
    def _():
        for cp in row_copies(t + 1, 1 - slot):
            cp.start()

    route = route_ref[...]
    y = route[:, 2:3] * yb[slot, 0] + route[:, 3:4] * yb[slot, 1]
    xn = x_ref[...] + g2_ref[0] * y
    if final:
        out_ref[...] = _rms(xn) * fg_ref[...]
    else:
        xo_ref[...] = xn
        h_ref[...] = _modulate_val(xn, ng_ref[...], sh_ref[0], sc_ref[0]).astype(h_ref.dtype)


def _combine(dm, y_buf, dest, x, route, mod3, gate_col, rows, *, final_g=None, next_g=None, next_mod3=None):
    tc = MOE_BLOCK
    n_blocks = rows // tc
    final = final_g is not None
    row_spec = pl.BlockSpec((tc, dm.d), lambda t, d_: (t, 0))
    vec_spec = pl.BlockSpec((1, dm.d), lambda t, d_: (0, 0))
    in_specs = [
        pl.BlockSpec(memory_space=pl.ANY),
        row_spec,
        pl.BlockSpec((tc, LANES), lambda t, d_: (t, 0)),
        dm.mod_spec(gate_col, tc),
    ]
    if final:
        in_specs += [vec_spec]
        args = (final_g.reshape(1, dm.d),)
        out_specs = row_spec
        out_shape = jax.ShapeDtypeStruct((rows, dm.d), F32)
    else:
        in_specs += [vec_spec, dm.mod_spec(0, tc), dm.mod_spec(1, tc)]
        args = (next_g.reshape(1, dm.d), next_mod3, next_mod3)
        out_specs = [row_spec, row_spec]
        out_shape = [jax.ShapeDtypeStruct((rows, dm.d), F32), jax.ShapeDtypeStruct((rows, dm.d), BF16)]
    grid_spec = pltpu.PrefetchScalarGridSpec(
        num_scalar_prefetch=1,
        grid=(n_blocks,),
        in_specs=in_specs,
        out_specs=out_specs,
        scratch_shapes=[
            pltpu.VMEM((2, 2, tc, dm.d), F32),
            pltpu.SemaphoreType.DMA((2,)),
        ],
    )
    return pl.pallas_call(
        functools.partial(_combine_kernel, tc=tc, n_blocks=n_blocks, final=final),
        grid_spec=grid_spec,
        out_shape=out_shape,
        compiler_params=_cparams("arbitrary"),
        name="moe_combine",
    )(dest, y_buf, x, route, mod3, *args)


def _dkv_kernel(a_ref, wq_ref, wkv_ref, gq_ref, gkv_ref, cos_ref, sin_ref, zq_ref, ckv_ref, kr_ref, *, kv_rank):
    a = a_ref[...]
    zq_ref[...] = (_rms(_dot(a, wq_ref[...])) * gq_ref[...]).astype(zq_ref.dtype)
    zkv = _dot(a, wkv_ref[...])
    ckv_ref[...] = (_rms(zkv[:, :kv_rank]) * gkv_ref[...]).astype(ckv_ref.dtype)
    kr_ref[...] = _rope128(zkv[:, kv_rank:], cos_ref[...], sin_ref[...]).astype(kr_ref.dtype)


def _dkv_proj(dm, h, wq, wkv, gq, gkv, cos_t, sin_t):
    tm = dm.tm
    q_rank = wq.shape[1]
    kv_rank = wkv.shape[1] - LANES
    rope_spec = pl.BlockSpec((tm, LANES), lambda i: (dm.ropeblk(i, tm), 0))
    return pl.pallas_call(
        functools.partial(_dkv_kernel, kv_rank=kv_rank),
        grid=(dm.t_all // tm,),
        in_specs=[
            pl.BlockSpec((tm, dm.d), lambda i: (i, 0)),
            pl.BlockSpec((dm.d, q_rank), lambda i: (0, 0)),
            pl.BlockSpec((dm.d, kv_rank + LANES), lambda i: (0, 0)),
            pl.BlockSpec((1, q_rank), lambda i: (0, 0)),
            pl.BlockSpec((1, kv_rank), lambda i: (0, 0)),
            rope_spec, rope_spec,
        ],
        out_specs=[
            pl.BlockSpec((tm, q_rank), lambda i: (i, 0)),
            pl.BlockSpec((tm, kv_rank), lambda i: (i, 0)),
            pl.BlockSpec((tm, LANES), lambda i: (i, 0)),
        ],
        out_shape=[
            jax.ShapeDtypeStruct((dm.t_all, q_rank), BF16),
            jax.ShapeDtypeStruct((dm.t_all, kv_rank), BF16),
            jax.ShapeDtypeStruct((dm.t_all, LANES), BF16),
        ],
        compiler_params=_cparams("arbitrary"),
        name="mla_down_proj",
    )(h, wq, wkv, gq.reshape(1, -1), gkv.reshape(1, -1), cos_t, sin_t)


def _qup_kernel(a_ref, wn_ref, wr_ref, cos_ref, sin_ref, o_ref, *, qscale, heads):
    a = a_ref[...]
    acc_n = _dot(a, wn_ref[...])
    acc_r = _dot(a, wr_ref[...])
    cos, sin = cos_ref[...], sin_ref[...]
    lo_mask = lax.broadcasted_iota(jnp.int32, (1, LANES), 1) < A_HEAD_DIM
    for h in range(heads):
        o_ref[:, h * 2 * LANES: h * 2 * LANES + LANES] = (acc_n[:, h * LANES:(h + 1) * LANES] * qscale).astype(
            o_ref.dtype)
    for g in range(heads // 2):
        r = _rope128(acc_r[:, g * LANES:(g + 1) * LANES], cos, sin) * qscale
        first = (2 * g) * 2 * LANES + LANES
        second = (2 * g + 1) * 2 * LANES + LANES
        o_ref[:, first: first + LANES] = jnp.where(lo_mask, r, 0.0).astype(o_ref.dtype)
        o_ref[:, second: second + LANES] = jnp.where(lo_mask, pltpu.roll(r, A_HEAD_DIM, 1), 0.0).astype(o_ref.dtype)


def _qup_proj(dm, zq, w_nope, w_rope, cos_t, sin_t, qscale, n_heads):
    tm = dm.tm
    k = zq.shape[1]
    hb = math.gcd(8, n_heads)
    assert hb % 2 == 0
    rope_spec = pl.BlockSpec((tm, LANES), lambda i, j: (dm.ropeblk(i, tm), 0))
    return pl.pallas_call(
        functools.partial(_qup_kernel, qscale=qscale, heads=hb),
        grid=(dm.t_lat // tm, n_heads // hb),
        in_specs=[
            pl.BlockSpec((tm, k), lambda i, j: (i, 0)),
            pl.BlockSpec((k, hb * LANES), lambda i, j: (0, j)),
            pl.BlockSpec((k, hb * A_HEAD_DIM), lambda i, j: (0, j)),
            rope_spec, rope_spec,
        ],
        out_specs=pl.BlockSpec((tm, hb * 2 * LANES), lambda i, j: (i, j)),
        out_shape=jax.ShapeDtypeStruct((dm.t_lat, n_heads * 2 * LANES), BF16),
        compiler_params=_cparams("arbitrary", "arbitrary"),
        name="mla_q_up",
    )(zq, w_nope, w_rope, cos_t, sin_t)


def _kvup_kernel(a_ref, wk_ref, wvt_ref, kr_ref, k_ref, vt_ref, *, heads):
    a = a_ref[...]
    kn = _dot(a, wk_ref[...])
    kr = kr_ref[...]
    for h in range(heads):
        k_ref[:, h * 2 * LANES: h * 2 * LANES + LANES] = kn[:, h * LANES: (h + 1) * LANES].astype(k_ref.dtype)
        k_ref[:, h * 2 * LANES + LANES: (h + 1) * 2 * LANES] = kr
    vt_ref[...] = _dot_nt(wvt_ref[...], a).astype(vt_ref.dtype)


def _kvup_proj(dm, ckv, w_uk, w_uvt, kr, n_heads):
    tm = dm.tm
    kv_rank = ckv.shape[1]
    hb = math.gcd(16, n_heads)
    return pl.pallas_call(
        functools.partial(_kvup_kernel, heads=hb),
        grid=(dm.t_all // tm, n_heads // hb),
        in_specs=[
            pl.BlockSpec((tm, kv_rank), lambda i, j: (i, 0)),
            pl.BlockSpec((kv_rank, hb * LANES), lambda i, j: (0, j)),
            pl.BlockSpec((hb * LANES, kv_rank), lambda i, j: (j, 0)),
            pl.BlockSpec((tm, LANES), lambda i, j: (i, 0)),
        ],
        out_specs=[
            pl.BlockSpec((tm, hb * 2 * LANES), lambda i, j: (i, j)),
            pl.BlockSpec((hb * LANES, tm), lambda i, j: (j, i)),
        ],
        out_shape=[
            jax.ShapeDtypeStruct((dm.t_all, n_heads * 2 * LANES), BF16),
            jax.ShapeDtypeStruct((n_heads * LANES, dm.t_all), BF16),
        ],
        compiler_params=_cparams("arbitrary", "arbitrary"),
        name="mla_kv_up",
    )(ckv, w_uk, w_uvt, kr)


def _mla_flash_kernel(q_ref, kl_ref, kc_ref, vl_ref, vc_ref, o_ref, m_sc, l_sc, acc_sc, sa_sc, sb_sc, pa_sc, pb_sc,
                      qt_sc, *, tk, n_chunks):
    tq = q_ref.shape[0]
    half = tq // 2
    n_ctx = kc_ref.shape[0]
    s_bufs, p_bufs = (sa_sc, sb_sc), (pa_sc, pb_sc)
    qt_sc[...] = q_ref[...].T

    def rows(ci):
        return tk if ci < n_chunks else n_ctx

    def keys(ci):
        return kl_ref[ci * tk:(ci + 1) * tk, :] if ci < n_chunks else kc_ref[...]

    def values_t(ci):
        return vl_ref[:, ci * tk:(ci + 1) * tk] if ci < n_chunks else vc_ref[...]

    n_all = n_chunks + 1
    s_bufs[0][:rows(0), :] = _dot(keys(0), qt_sc[...])
    for ci in range(n_all):
        for hf in range(2):
            sl = slice(hf * half, (hf + 1) * half)
            if ci > 0:
                pv = _dot(values_t(ci - 1), p_bufs[(ci - 1) % 2][:rows(ci - 1), sl])
            if ci + 1 < n_all:
                s_bufs[(ci + 1) % 2][:rows(ci + 1), sl] = _dot(keys(ci + 1), qt_sc[:, sl])
            s = s_bufs[ci % 2][:rows(ci), sl]
            m_cur = jnp.max(s, axis=0, keepdims=True)
            if ci == 0:
                m_new = m_cur
            else:
                m_prev = m_sc[:, sl]
                m_new = jnp.maximum(m_prev, m_cur)
                alpha = jnp.exp2(m_prev - m_new)
            p = jnp.exp2(s - m_new)
            p_sum = jnp.sum(p, axis=0, keepdims=True)
            l_sc[:, sl] = p_sum if ci == 0 else alpha * l_sc[:, sl] + p_sum
            m_sc[:, sl] = m_new
            p_bufs[ci % 2][:rows(ci), sl] = p.astype(BF16)
            if ci == 1:
                acc_sc[:, sl] = alpha * pv
            elif ci > 1:
                acc_sc[:, sl] = alpha * (acc_sc[:, sl] + pv)
    pv = _dot(values_t(n_all - 1), p_bufs[(n_all - 1) % 2][:rows(n_all - 1), :])
    o_ref[...] = ((acc_sc[...] + pv) / l_sc[...]).T.astype(o_ref.dtype)


def _mla_flash(dm, q, k, vt, n_heads):
    tq = min(512, dm.s)
    tk = min(512, dm.s // 2)
    nq = dm.s // tq
    n_chunks = dm.s // tk
    assert dm.c <= tk and tq % (2 * LANES) == 0
    return pl.pallas_call(
        functools.partial(_mla_flash_kernel, tk=tk, n_chunks=n_chunks),
        grid=(dm.b, n_heads, nq),
        in_specs=[
            pl.BlockSpec((tq, 2 * LANES), lambda b, h, i: (b * nq + i, h)),
            pl.BlockSpec((dm.s, 2 * LANES), lambda b, h, i: (b, h)),
            pl.BlockSpec((dm.c, 2 * LANES), lambda b, h, i: (dm.t_lat // dm.c + b, h)),
            pl.BlockSpec((LANES, dm.s), lambda b, h, i: (h, b)),
            pl.BlockSpec((LANES, dm.c), lambda b, h, i: (h, dm.t_lat // dm.c + b)),
        ],
        out_specs=pl.BlockSpec((tq, LANES), lambda b, h, i: (b * nq + i, h)),
        out_shape=jax.ShapeDtypeStruct((dm.t_lat, n_heads * LANES), BF16),
        scratch_shapes=[
            pltpu.VMEM((1, tq), F32),
            pltpu.VMEM((1, tq), F32),
            pltpu.VMEM((LANES, tq), F32),
            pltpu.VMEM((tk, tq), F32),
            pltpu.VMEM((tk, tq), F32),
            pltpu.VMEM((tk, tq), BF16),
            pltpu.VMEM((tk, tq), BF16),
            pltpu.VMEM((2 * LANES, tq), BF16),
        ],
        compiler_params=_cparams("arbitrary", "arbitrary", "arbitrary"),
        name="mla_flash",
    )(q, k, k, vt, vt)


def _mixer_a(dm, h, w_qkv, sink, cos_t, sin_t):
    n_heads = sink.shape[0]
    n_kv = (w_qkv.shape[1] // A_HEAD_DIM - n_heads) // 2
    assert (n_heads // n_kv) % 2 == 0 and n_kv % 2 == 0
    z = _qkv_proj(dm, h, w_qkv.astype(BF16), cos_t, sin_t, n_heads * A_HEAD_DIM, n_kv * A_HEAD_DIM)
    return _attn_a(dm, z, sink, n_heads, n_kv)


def _mixer_b(dm, h, w_dkv, g_q, g_kv, w_uq, w_ukv, wo_rows, cos_t, sin_t):
    q_rank, kv_rank = g_q.shape[0], g_kv.shape[0]
    rope = w_dkv.shape[1] - q_rank - kv_rank
    n_heads = (w_uq.shape[1] - w_ukv.shape[1] + wo_rows) // rope
    nope = (w_ukv.shape[1] - wo_rows) // n_heads
    d_v = wo_rows // n_heads
    assert rope == A_HEAD_DIM and nope == LANES and d_v == LANES
    wq = w_dkv[:, :q_rank].astype(BF16)
    wkv = jnp.pad(w_dkv[:, q_rank:], ((0, 0), (0, LANES - rope))).astype(BF16)
    zq, ckv, kr = _dkv_proj(dm, h, wq, wkv, g_q, g_kv, cos_t, sin_t)
    w_uq3 = w_uq.reshape(q_rank, n_heads, nope + rope)
    w_uq_n = w_uq3[..., :nope].reshape(q_rank, n_heads * nope).astype(BF16)
    w_uq_r = w_uq3[..., nope:].reshape(q_rank, n_heads * rope).astype(BF16)
    qscale = (nope + rope) ** -0.5 * LOG2E
    q = _qup_proj(dm, zq, w_uq_n, w_uq_r, cos_t, sin_t, qscale, n_heads)
    w_ukv3 = w_ukv.reshape(kv_rank, n_heads, nope + d_v)
    w_uk = w_ukv3[..., :nope].reshape(kv_rank, n_heads * nope).astype(BF16)
    w_uvt = jnp.transpose(w_ukv3[..., nope:], (1, 2, 0)).reshape(n_heads * d_v, kv_rank).astype(BF16)
    k, vt = _kvup_proj(dm, ckv, w_uk, w_uvt, kr, n_heads)
    return _mla_flash(dm, q, k, vt, n_heads)


def kernel(x, c, ctx, c_ctx, ada_w, ada_b, norm_g, final_g, a_wqkv, a_wo, a_sink, b_wdkv, b_gq, b_gkv, b_wuq,
           b_wukv, b_wo, r_wg, r_bg, r_we, r_be, e_wgu, e_wdn):
    b, s, d = x.shape
    c_len = ctx.shape[1]
    depth = ada_w.shape[0]
    dm = _Dims(b, s, c_len, d)
    assert b + 1 <= MOD_ROWS and dm.t_lat % c_len == 0 and c_len % WINDOW == 0 and s % GRID_W == 0
    n_groups, n_experts = r_wg.shape[2], r_we.shape[2]
    assert n_groups + n_experts <= LANES

    cc = jnp.concatenate([c, c_ctx[None, :], jnp.zeros((MOD_ROWS - b - 1, d), F32)], axis=0)
    mod = _ada(cc, ada_w, ada_b)
    mod3 = [mod[i].reshape(MOD_ROWS, 1, 6 * d) for i in range(depth)]
    cos_t, sin_t = _rope_tables(s, A_HEAD_DIM, dm.tm)

    xa = (x.reshape(dm.t_lat, d), ctx.reshape(dm.t_ctx, d))
    h = xa + (norm_g[0, 0], mod3[0])
    out = None
    for i in range(depth):
        last = i == depth - 1
        j = i // 2
        rows = dm.t_lat if last else dm.t_all
        if i % 2 == 0:
            o = _mixer_a(dm, h, a_wqkv[j], a_sink[j], cos_t, sin_t)
            w_o = a_wo[j]
        else:
            o = _mixer_b(dm, h, b_wdkv[j], b_gq[j], b_gkv[j], b_wuq[j], b_wukv[j], b_wo.shape[1], cos_t, sin_t)
            w_o = b_wo[j]
        x1 = _mm_res(dm, o, w_o.astype(BF16), xa, mod3[i], 2, rows)
        w_r = jnp.pad(jnp.concatenate([r_wg[i], r_we[i]], axis=1), ((0, 0), (0, LANES - n_groups - n_experts)))
        b_r = jnp.pad(jnp.concatenate([r_bg[i], r_be[i]]), (0, LANES - n_groups - n_experts)).reshape(1, LANES)
        h2, route = _mod_router(dm, x1, norm_g[i, 1], mod3[i], 3, 4, w_r, b_r, rows, n_groups, n_experts)
        blk_e, nxt_e, src_tok, n_used, dest, n_rows = _dispatch(route, n_experts)
        y_buf = _moe_experts(h2, blk_e, nxt_e, src_tok, n_used, e_wgu, e_wdn, i, n_rows)
        if last:
            out = _combine(dm, y_buf, dest, x1, route, mod3[i], 5, rows, final_g=final_g)
        else:
            xa, h = _combine(dm, y_buf, dest, x1, route, mod3[i], 5, rows,
                             next_g=norm_g[i + 1, 0], next_mod3=mod3[i + 1])
    return out.reshape(b, s, d)
```
